```python
import jax, jax.numpy as jnp
from jax import lax
import numpy as np

D_MODEL = 1024
BATCH = 16
SEQ = 2048
DEPTH = 2
DEC_BATCH = 16
DEC_SEQ = 64
PAST_LEN = 1024

CHUNK = 64
Q_BLOCK = 128
N_MIXERS = 2
N_FOX_LAYERS = (DEPTH + 1) // 2
N_MLA_LAYERS = DEPTH // 2
FOX_HEADS = 16
FOX_HEAD_DIM = D_MODEL // FOX_HEADS
MLA_HEADS = 16
MLA_NOPE = 64
MLA_ROPE = 32
MLA_V = 64
MLA_Q_LORA = 384
MLA_KV_LORA = 256
ROPE_THETA = 10000.0
N_EXPERTS = 32
TOP_K = 4
D_FF = D_MODEL
SWIGLU_LIMIT = 7.0
SWIGLU_ALPHA = 1.702
EXPERT_BLOCK = 256
DEEPNORM_ALPHA = (2.0 * DEPTH) ** 0.25
DEEPNORM_BETA = (8.0 * DEPTH) ** -0.25
LN_EPS = 1e-5
RMS_EPS = 1e-6
NEG_INF = -1e30

kernel_name = 'chunk_stream_fox_mla_moe'


def _layernorm(x, g, b):
    xf = x.astype(jnp.float32)
    mu = jnp.mean(xf, -1, keepdims=True)
    var = jnp.mean(jnp.square(xf - mu), -1, keepdims=True)
    return ((xf - mu) * lax.rsqrt(var + LN_EPS) * g + b).astype(x.dtype)


def _rmsnorm(x, g):
    xf = x.astype(jnp.float32)
    return (xf * lax.rsqrt(jnp.mean(jnp.square(xf), -1, keepdims=True) + RMS_EPS) * g).astype(x.dtype)


def _rope(x, pos):
    half = x.shape[-1] // 2
    inv_freq = ROPE_THETA ** (-jnp.arange(half, dtype=jnp.float32) / half)
    ang = pos.astype(jnp.float32)[:, None] * inv_freq[None, :]
    ang = ang.reshape((ang.shape[0],) + (1,) * (x.ndim - 3) + (half,))
    cos, sin = jnp.cos(ang), jnp.sin(ang)
    xf = x.astype(jnp.float32)
    x1, x2 = xf[..., :half], xf[..., half:]
    return jnp.concatenate([x1 * cos - x2 * sin, x1 * sin + x2 * cos], -1).astype(x.dtype)


def _causal(q_pos, k_pos):
    return k_pos[None, :] <= q_pos[:, None]


def _chunk_causal(q_pos, k_pos):
    return (k_pos[None, :] // CHUNK) <= (q_pos[:, None] // CHUNK)


def _forget_bias(cq, ck):
    return jnp.transpose(cq, (0, 2, 1))[..., :, None] - jnp.transpose(ck, (0, 2, 1))[..., None, :]


def _attend(q, k, v, allowed, bias=None):
    s = jnp.einsum('bqhd,bkhd->bhqk', q, k).astype(jnp.float32) * (q.shape[-1] ** -0.5)
    if bias is not None:
        s = s + bias
    s = jnp.where(allowed, s, NEG_INF)
    p = jax.nn.softmax(s, axis=-1)
    return jnp.einsum('bhqk,bkhd->bqhd', p.astype(v.dtype), v)


def _blocked_attention(q, k, v, mask_fn, cq=None, ck=None):
    b, t, h, _ = q.shape
    k_pos = jnp.arange(k.shape[1])

    def one_block(i):
        start = i * Q_BLOCK
        qb = lax.dynamic_slice_in_dim(q, start, Q_BLOCK, axis=1)
        allowed = mask_fn(start + jnp.arange(Q_BLOCK), k_pos)
        bias = None
        if cq is not None:
            bias = _forget_bias(lax.dynamic_slice_in_dim(cq, start, Q_BLOCK, axis=1), ck)
        return _attend(qb, k, v, allowed, bias)

    out = lax.map(one_block, jnp.arange(t // Q_BLOCK))
    return jnp.moveaxis(out, 0, 1).reshape(b, t, h, v.shape[-1])


def _fox_project(x, w_qkv, w_f, b_f):
    b, t, _ = x.shape
    qkv = (x @ w_qkv).reshape(b, t, 3, FOX_HEADS, FOX_HEAD_DIM)
    logf = jax.nn.log_sigmoid((x @ w_f + b_f).astype(jnp.float32))
    return qkv[:, :, 0], qkv[:, :, 1], qkv[:, :, 2], logf


def _fox_prompt(x, w_qkv, w_f, b_f, w_o):
    b, t, _ = x.shape
    q, k, v, logf = _fox_project(x, w_qkv, w_f, b_f)
    c = jnp.cumsum(logf, axis=1)
    o = _blocked_attention(q, k, v, _causal, c, c)
    return o.reshape(b, t, -1) @ w_o, k, v, logf


def _fox_sample(x, cache_k, cache_v, cache_logf, w_qkv, w_f, b_f, w_o):
    b, t, _ = x.shape
    past = cache_k.shape[1]
    q, k, v, logf = _fox_project(x, w_qkv, w_f, b_f)
    k_all = jnp.concatenate([cache_k.astype(k.dtype), k], 1)
    v_all = jnp.concatenate([cache_v.astype(v.dtype), v], 1)
    c = jnp.cumsum(jnp.concatenate([cache_logf.astype(jnp.float32), logf], 1), axis=1)
    allowed = _causal(past + jnp.arange(t), jnp.arange(past + t))
    o = _attend(q, k_all, v_all, allowed, _forget_bias(c[:, past:], c))
    return o.reshape(b, t, -1) @ w_o, k, v, logf


def _mla_project(x, pos, w_dq, g_q, w_uq, w_dkv, g_kv):
    b, t, _ = x.shape
    q = (_rmsnorm(x @ w_dq, g_q) @ w_uq).reshape(b, t, MLA_HEADS, MLA_NOPE + MLA_ROPE)
    q = jnp.concatenate([q[..., :MLA_NOPE], _rope(q[..., MLA_NOPE:], pos)], -1)
    kv = x @ w_dkv
    ckv = _rmsnorm(kv[..., :MLA_KV_LORA], g_kv)
    kpe = _rope(kv[..., MLA_KV_LORA:], pos)
    return q, ckv, kpe


def _mla_expand(ckv, kpe, w_ukv):
    b, t, _ = ckv.shape
    kv = (ckv @ w_ukv).reshape(b, t, MLA_HEADS, MLA_NOPE + MLA_V)
    k = jnp.concatenate([kv[..., :MLA_NOPE], jnp.broadcast_to(kpe[:, :, None, :], (b, t, MLA_HEADS, MLA_ROPE))], -1)
    return k, kv[..., MLA_NOPE:]


def _mla_prompt(x, w_dq, g_q, w_uq, w_dkv, g_kv, w_ukv, w_o):
    b, t, _ = x.shape
    q, ckv, kpe = _mla_project(x, jnp.arange(t), w_dq, g_q, w_uq, w_dkv, g_kv)
    k, v = _mla_expand(ckv, kpe, w_ukv)
    o = _blocked_attention(q, k, v, _chunk_causal)
    return o.reshape(b, t, -1) @ w_o, ckv, kpe


def _mla_sample(x, cache_ckv, cache_kpe, w_dq, g_q, w_uq, w_dkv, g_kv, w_ukv, w_o):
    b, t, _ = x.shape
    past = cache_ckv.shape[1]
    q, ckv, kpe = _mla_project(x, past + jnp.arange(t), w_dq, g_q, w_uq, w_dkv, g_kv)
    k, v = _mla_expand(jnp.concatenate([cache_ckv.astype(ckv.dtype), ckv], 1),
                       jnp.concatenate([cache_kpe.astype(kpe.dtype), kpe], 1), w_ukv)
    o = _attend(q, k, v, _chunk_causal(past + jnp.arange(t), jnp.arange(past + t)))
    return o.reshape(b, t, -1) @ w_o, ckv, kpe


def _moe(x, w_r, b_r, w_gu, b_gu, w_dn, b_dn):
    b, t, d = x.shape
    xt = x.reshape(-1, d)
    n = xt.shape[0]
    logits = (xt @ w_r + b_r).astype(jnp.float32)
    top_v, top_i = lax.top_k(logits, TOP_K)
    gates = jax.nn.softmax(top_v, axis=-1)
    p = n * TOP_K
    flat_e = top_i.reshape(p).astype(jnp.int32)
    flat_tok = jnp.arange(p, dtype=jnp.int32) // TOP_K
    flat_g = gates.reshape(p)
    order = jnp.argsort(flat_e)
    se = flat_e[order]
    counts = jnp.bincount(flat_e, length=N_EXPERTS)
    starts = jnp.cumsum(counts) - counts
    pcounts = (counts + EXPERT_BLOCK - 1) // EXPERT_BLOCK * EXPERT_BLOCK
    pends = jnp.cumsum(pcounts)
    pstarts = pends - pcounts
    dest = pstarts[se] + jnp.arange(p) - starts[se]
    n_blocks = -(-p // EXPERT_BLOCK) + N_EXPERTS
    rows = n_blocks * EXPERT_BLOCK
    buf_tok = jnp.full((rows,), n, jnp.int32).at[dest].set(flat_tok[order])
    buf_g = jnp.zeros((rows,), jnp.float32).at[dest].set(flat_g[order])
    block_e = jnp.minimum(jnp.searchsorted(pends, jnp.arange(n_blocks) * EXPERT_BLOCK, side='right'), N_EXPERTS - 1)
    x_pad = jnp.concatenate([xt, jnp.zeros((1, d), xt.dtype)], 0)
    xb = x_pad[buf_tok].reshape(n_blocks, EXPERT_BLOCK, d)

    def expert_block(args):
        xr, e = args
        gu = xr @ w_gu[e] + b_gu[e]
        g = jnp.minimum(gu[:, :D_FF], SWIGLU_LIMIT)
        u = jnp.clip(gu[:, D_FF:], -SWIGLU_LIMIT, SWIGLU_LIMIT)
        h = g * jax.nn.sigmoid(SWIGLU_ALPHA * g) * (u + 1)
        return h @ w_dn[e] + b_dn[e]

    yb = lax.map(expert_block, (xb, block_e)).reshape(rows, d)
    y = jnp.zeros((n + 1, d), jnp.float32).at[buf_tok].add(yb.astype(jnp.float32) * buf_g[:, None])[:n]
    return y.astype(x.dtype).reshape(b, t, d)


def setup_inputs(seed: int = 0) -> dict:
    key = jax.random.key(seed)
    ks = iter(jax.random.split(key, 40))
    f32 = jnp.float32

    def nrm(shape, scale=1.0):
        return jax.random.normal(next(ks), shape, f32) * scale

    return {
        'x_prompt': nrm((BATCH, SEQ, D_MODEL)),
        'x_sample': nrm((DEC_BATCH, DEC_SEQ, D_MODEL)),
        'cache_fox_k': nrm((N_FOX_LAYERS, DEC_BATCH, PAST_LEN, FOX_HEADS, FOX_HEAD_DIM)),
        'cache_fox_v': nrm((N_FOX_LAYERS, DEC_BATCH, PAST_LEN, FOX_HEADS, FOX_HEAD_DIM)),
        'cache_fox_logf': jax.nn.log_sigmoid(nrm((N_FOX_LAYERS, DEC_BATCH, PAST_LEN, FOX_HEADS), 0.5) + 2.5),
        'cache_mla_ckv': nrm((N_MLA_LAYERS, DEC_BATCH, PAST_LEN, MLA_KV_LORA)),
        'cache_mla_kpe': nrm((N_MLA_LAYERS, DEC_BATCH, PAST_LEN, MLA_ROPE)),
        'fox_w_qkv': nrm((N_FOX_LAYERS, D_MODEL, 3 * FOX_HEADS * FOX_HEAD_DIM), D_MODEL ** -0.5),
        'fox_w_f': nrm((N_FOX_LAYERS, D_MODEL, FOX_HEADS), D_MODEL ** -0.5),
        'fox_b_f': jax.random.uniform(next(ks), (N_FOX_LAYERS, FOX_HEADS), f32, 1.0, 4.0),
        'fox_w_o': nrm((N_FOX_LAYERS, FOX_HEADS * FOX_HEAD_DIM, D_MODEL), DEEPNORM_BETA * (FOX_HEADS * FOX_HEAD_DIM) ** -0.5),
        'mla_w_dq': nrm((N_MLA_LAYERS, D_MODEL, MLA_Q_LORA), D_MODEL ** -0.5),
        'mla_g_q': 1.0 + nrm((N_MLA_LAYERS, MLA_Q_LORA), 0.01),
        'mla_w_uq': nrm((N_MLA_LAYERS, MLA_Q_LORA, MLA_HEADS * (MLA_NOPE + MLA_ROPE)), MLA_Q_LORA ** -0.5),
        'mla_w_dkv': nrm((N_MLA_LAYERS, D_MODEL, MLA_KV_LORA + MLA_ROPE), D_MODEL ** -0.5),
        'mla_g_kv': 1.0 + nrm((N_MLA_LAYERS, MLA_KV_LORA), 0.01),
        'mla_w_ukv': nrm((N_MLA_LAYERS, MLA_KV_LORA, MLA_HEADS * (MLA_NOPE + MLA_V)), MLA_KV_LORA ** -0.5),
        'mla_w_o': nrm((N_MLA_LAYERS, MLA_HEADS * MLA_V, D_MODEL), DEEPNORM_BETA * (MLA_HEADS * MLA_V) ** -0.5),
        'ln_g': 1.0 + nrm((DEPTH, 2, D_MODEL), 0.01),
        'ln_b': nrm((DEPTH, 2, D_MODEL), 0.01),
        'moe_w_router': nrm((DEPTH, D_MODEL, N_EXPERTS), D_MODEL ** -0.5),
        'moe_b_router': nrm((DEPTH, N_EXPERTS), 0.01),
        'moe_w_gu': nrm((DEPTH, N_EXPERTS, D_MODEL, 2 * D_FF), D_MODEL ** -0.5),
        'moe_b_gu': nrm((DEPTH, N_EXPERTS, 2 * D_FF), 0.01),
        'moe_w_down': nrm((DEPTH, N_EXPERTS, D_FF, D_MODEL), DEEPNORM_BETA * D_FF ** -0.5),
        'moe_b_down': nrm((DEPTH, N_EXPERTS, D_MODEL), 0.01),
    }


def reference(x_prompt, x_sample, cache_fox_k, cache_fox_v, cache_fox_logf, cache_mla_ckv, cache_mla_kpe,
              fox_w_qkv, fox_w_f, fox_b_f, fox_w_o,
              mla_w_dq, mla_g_q, mla_w_uq, mla_w_dkv, mla_g_kv, mla_w_ukv, mla_w_o,
              ln_g, ln_b, moe_w_router, moe_b_router, moe_w_gu, moe_b_gu, moe_w_down, moe_b_down):
    xp, xs = x_prompt, x_sample
    fox_p, fox_s, mla_p, mla_s = [], [], [], []
    for layer in range(DEPTH):
        j = layer // N_MIXERS
        if layer % N_MIXERS == 0:
            mp, kp, vp, fp = _fox_prompt(xp, fox_w_qkv[j], fox_w_f[j], fox_b_f[j], fox_w_o[j])
            ms, k_s, v_s, f_s = _fox_sample(xs, cache_fox_k[j], cache_fox_v[j], cache_fox_logf[j],
                                            fox_w_qkv[j], fox_w_f[j], fox_b_f[j], fox_w_o[j])
            fox_p.append((kp, vp, fp))
            fox_s.append((k_s, v_s, f_s))
        else:
            mla_w = (mla_w_dq[j], mla_g_q[j], mla_w_uq[j], mla_w_dkv[j], mla_g_kv[j], mla_w_ukv[j], mla_w_o[j])
            mp, cp, ep = _mla_prompt(xp, *mla_w)
            ms, c_s, e_s = _mla_sample(xs, cache_mla_ckv[j], cache_mla_kpe[j], *mla_w)
            mla_p.append((cp, ep))
            mla_s.append((c_s, e_s))
        xp = _layernorm(DEEPNORM_ALPHA * xp + mp, ln_g[layer, 0], ln_b[layer, 0])
        xs = _layernorm(DEEPNORM_ALPHA * xs + ms, ln_g[layer, 0], ln_b[layer, 0])
        moe_w = (moe_w_router[layer], moe_b_router[layer], moe_w_gu[layer], moe_b_gu[layer],
                 moe_w_down[layer], moe_b_down[layer])
        xp = _layernorm(DEEPNORM_ALPHA * xp + _moe(xp, *moe_w), ln_g[layer, 1], ln_b[layer, 1])
        xs = _layernorm(DEEPNORM_ALPHA * xs + _moe(xs, *moe_w), ln_g[layer, 1], ln_b[layer, 1])

    new_fox_k_prompt = jnp.stack([e[0] for e in fox_p])
    new_fox_v_prompt = jnp.stack([e[1] for e in fox_p])
    new_fox_logf_prompt = jnp.stack([e[2] for e in fox_p])
    new_mla_ckv_prompt = jnp.stack([e[0] for e in mla_p])
    new_mla_kpe_prompt = jnp.stack([e[1] for e in mla_p])
    new_fox_k_sample = jnp.stack([e[0] for e in fox_s])
    new_fox_v_sample = jnp.stack([e[1] for e in fox_s])
    new_fox_logf_sample = jnp.stack([e[2] for e in fox_s])
    new_mla_ckv_sample = jnp.stack([e[0] for e in mla_s])
    new_mla_kpe_sample = jnp.stack([e[1] for e in mla_s])
    return (xp, xs,
            new_fox_k_prompt, new_fox_v_prompt, new_fox_logf_prompt, new_mla_ckv_prompt, new_mla_kpe_prompt,
            new_fox_k_sample, new_fox_v_sample, new_fox_logf_sample, new_mla_ckv_sample, new_mla_kpe_sample)
```

```python
import functools

import numpy as np
import jax
import jax.numpy as jnp
from jax import lax
from jax.experimental import pallas as pl
from jax.experimental.pallas import tpu as pltpu

F32 = jnp.float32
BF16 = jnp.bfloat16
I32 = jnp.int32

DEPTH = 2
CHUNK = 64
CHUNK_LOG2 = 6
assert 1 << CHUNK_LOG2 == CHUNK
MLA_NOPE = 64
MLA_ROPE = 32
MLA_V = 64
ROPE_THETA = 10000.0
TOP_K = 4
SWIGLU_LIMIT = 7.0
SWIGLU_ALPHA = 1.702
DEEPNORM_ALPHA = (2.0 * DEPTH) ** 0.25
LN_EPS = 1e-5
RMS_EPS = 1e-6
NEG_INF = -1e30

LANES = 128
AUG_COLS = 6
MIB = 1 << 20

ROW_TILE = 256
MOE_BLOCK = 256
ATTN_TILE = 512


def _cparams(n_axes, vmem_mib):
    return pltpu.CompilerParams(dimension_semantics=("arbitrary",) * n_axes,
                                vmem_limit_bytes=vmem_mib * MIB)


def _full(shape):
    nd = len(shape)
    return pl.BlockSpec(shape, lambda *_: (0,) * nd)


def _rows(tm, width):
    return pl.BlockSpec((tm, width), lambda i: (i, 0))


def _rows_a(tm, width, nblk_a):
    return pl.BlockSpec((tm, width), lambda i: (jnp.minimum(i, nblk_a - 1), 0))


def _rows_b(tm, width, nblk_a, base):
    return pl.BlockSpec((tm, width), lambda i: (base + jnp.maximum(i - nblk_a, 0), 0))


def _split2(x):
    hi = x.astype(BF16)
    lo = (x - hi.astype(F32)).astype(BF16)
    return hi, lo


def _split3(x):
    hi = x.astype(BF16)
    r = x - hi.astype(F32)
    mid = r.astype(BF16)
    lo = (r - mid.astype(F32)).astype(BF16)
    return hi, mid, lo


def _dot(a, b):
    return jnp.dot(a, b, preferred_element_type=F32)


def _layernorm(z, g, b):
    mu = jnp.mean(z, axis=-1, keepdims=True)
    zc = z - mu
    var = jnp.mean(zc * zc, axis=-1, keepdims=True)
    return zc * lax.rsqrt(var + LN_EPS) * g + b


def _rmsnorm(x, g):
    return x * lax.rsqrt(jnp.mean(x * x, axis=-1, keepdims=True) + RMS_EPS) * g


def _fox_proj_kernel(nblk_p, xa_ref, xb_ref, wqkv_ref, wfh_ref, wfl_ref, bf_ref,
                     qkv_ref, kp_ref, ks_ref, vp_ref, vs_ref, logf_ref):
    i = pl.program_id(0)
    is_p = i < nblk_p
    x = jnp.where(is_p, xa_ref[...], xb_ref[...])
    xh, xl = _split2(x)
    qkv = _dot(xh, wqkv_ref[...])
    qkv_ref[...] = qkv.astype(BF16)
    d = kp_ref.shape[-1]
    k = qkv[:, d:2 * d]
    v = qkv[:, 2 * d:3 * d]

    @pl.when(is_p)
    def _():
        kp_ref[...] = k
        vp_ref[...] = v

    @pl.when(jnp.logical_not(is_p))
    def _():
        ks_ref[...] = k
        vs_ref[...] = v

    z = _dot(xh, wfh_ref[...]) + _dot(xl, wfh_ref[...]) + _dot(xh, wfl_ref[...]) + bf_ref[...]
    logf_ref[...] = -(jnp.maximum(-z, 0.0) + jnp.log1p(jnp.exp(-jnp.abs(z))))


def _fox_proj(xp, xs, w_qkv, w_f, b_f):
    n_p, d = xp.shape
    n_s = xs.shape[0]
    n_t = n_p + n_s
    h = w_f.shape[-1]
    tm = ROW_TILE
    nblk_p = n_p // tm
    wfh, wfl = _split2(w_f)
    return pl.pallas_call(
        functools.partial(_fox_proj_kernel, nblk_p),
        out_shape=(jax.ShapeDtypeStruct((n_t, 3 * d), BF16),
                   jax.ShapeDtypeStruct((n_p, d), F32), jax.ShapeDtypeStruct((n_s, d), F32),
                   jax.ShapeDtypeStruct((n_p, d), F32), jax.ShapeDtypeStruct((n_s, d), F32),
                   jax.ShapeDtypeStruct((n_t, h), F32)),
        grid=(n_t // tm,),
        in_specs=[_rows_a(tm, d, nblk_p), _rows_b(tm, d, nblk_p, 0),
                  _full((d, 3 * d)), _full((d, h)), _full((d, h)), _full((1, h))],
        out_specs=(_rows(tm, 3 * d),
                   _rows_a(tm, d, nblk_p), _rows_b(tm, d, nblk_p, 0),
                   _rows_a(tm, d, nblk_p), _rows_b(tm, d, nblk_p, 0),
                   _rows(tm, h)),
        compiler_params=_cparams(1, 48),
        name="fox_proj",
    )(xp, xs, w_qkv.astype(BF16), wfh, wfl, b_f.reshape(1, h))


def _fox_aug_kernel(chunk, logf_ref, place_ref, ones_ref, qaug_ref, kaug_ref):
    t, h = logf_ref.shape
    r = lax.broadcasted_iota(I32, (chunk, chunk), 0)
    c = lax.broadcasted_iota(I32, (chunk, chunk), 1)
    tri = jnp.where(c <= r, 1.0, 0.0).astype(BF16)
    carry = jnp.zeros((1, h), F32)
    for j in range(t // chunk):
        rows = pl.ds(j * chunk, chunk)
        hi, mid, lo = _split3(logf_ref[rows, :])
        cs = _dot(tri, hi) + _dot(tri, mid) + _dot(tri, lo) + carry
        carry = cs[chunk - 1:chunk, :]
        chi, cmid, clo = _split3(cs)
        qa = _dot(chi, place_ref[0]) + _dot(cmid, place_ref[1]) + _dot(clo, place_ref[2]) + ones_ref[0]
        ka = ones_ref[1] - (_dot(chi, place_ref[3]) + _dot(cmid, place_ref[4]) + _dot(clo, place_ref[5]))
        qaug_ref[rows, :] = qa.astype(BF16)
        kaug_ref[rows, :] = ka.astype(BF16)


def _fox_aug(logf3):
    b, t, h = logf3.shape
    assert h * AUG_COLS <= LANES
    chunk = 512 if t % 512 == 0 else LANES
    assert t % chunk == 0
    place = np.zeros((AUG_COLS, h, LANES), np.float32)
    ones = np.zeros((2, 1, LANES), np.float32)
    for hh in range(h):
        for j in range(AUG_COLS):
            place[j, hh, AUG_COLS * hh + j] = 1.0
        ones[0, 0, AUG_COLS * hh + 3:AUG_COLS * hh + 6] = 1.0
        ones[1, 0, AUG_COLS * hh:AUG_COLS * hh + 3] = 1.0
    return pl.pallas_call(
        functools.partial(_fox_aug_kernel, chunk),
        out_shape=(jax.ShapeDtypeStruct((b, t, LANES), BF16), jax.ShapeDtypeStruct((b, t, LANES), BF16)),
        grid=(b,),
        in_specs=[pl.BlockSpec((None, t, h), lambda i: (i, 0, 0)),
                  _full((AUG_COLS, h, LANES)), _full((2, 1, LANES))],
        out_specs=(pl.BlockSpec((None, t, LANES), lambda i: (i, 0, 0)),
                   pl.BlockSpec((None, t, LANES), lambda i: (i, 0, 0))),
        compiler_params=_cparams(1, 32),
        name="fox_aug",
    )(logf3, jnp.asarray(place, BF16), jnp.asarray(ones, F32))


def _attn_kernel(fox, tq, tk, q_off, *refs):
    if fox:
        q_ref, k_ref, v_ref, qaug_ref, kaug_ref, o_ref, m_ref, l_ref, acc_ref = refs
    else:
        q_ref, k_ref, v_ref, o_ref, m_ref, l_ref, acc_ref = refs
    pair = pl.program_id(1)
    iq = pl.program_id(2)
    q_first = q_off + iq * tq
    n_full = q_first // tk
    half = LANES // 2
    lane = lax.broadcasted_iota(I32, (1, LANES), 1)
    outs = []
    for hh in range(2):
        if fox:
            head = 2 * pair + hh
            qmask = (lane >= hh * half) & (lane < (hh + 1) * half)
            amask = (lane >= AUG_COLS * head) & (lane < AUG_COLS * head + AUG_COLS)
            scale = half ** -0.5
            qop = jnp.concatenate(
                [jnp.where(qmask, q_ref[...], 0.0).astype(BF16) * jnp.asarray(scale, BF16),
                 jnp.where(amask, qaug_ref[...], 0.0).astype(BF16)], axis=1)
        else:
            qop = q_ref[:, hh * LANES:(hh + 1) * LANES]

        m_ref[...] = jnp.full(m_ref.shape, NEG_INF, F32)
        l_ref[...] = jnp.zeros(l_ref.shape, F32)
        acc_ref[...] = jnp.zeros(acc_ref.shape, F32)

        def step(j, masked, qop=qop, hh=hh):
            rows = pl.ds(pl.multiple_of(j * tk, tk), tk)
            if fox:
                kop = jnp.concatenate([k_ref[rows, :], kaug_ref[rows, :]], axis=1)
            else:
                kop = k_ref[rows, hh * LANES:(hh + 1) * LANES]
            s = lax.dot_general(qop, kop, (((1,), (1,)), ((), ())), preferred_element_type=F32)
            if masked:
                qpos = q_first + lax.broadcasted_iota(I32, (tq, tk), 0)
                kpos = j * tk + lax.broadcasted_iota(I32, (tq, tk), 1)
                if fox:
                    allowed = kpos <= qpos
                else:
                    allowed = (lax.shift_right_logical(kpos, CHUNK_LOG2)
                               <= lax.shift_right_logical(qpos, CHUNK_LOG2))
                s = jnp.where(allowed, s, NEG_INF)
            m_prev = m_ref[...]
            m_new = jnp.maximum(m_prev, jnp.max(s, axis=1, keepdims=True))
            alpha = jnp.exp(m_prev - m_new)
            p = jnp.exp(s - m_new)
            l_ref[...] = alpha * l_ref[...] + jnp.sum(p, axis=1, keepdims=True)
            acc_ref[...] = alpha * acc_ref[...] + _dot(p.astype(BF16), v_ref[rows, :])
            m_ref[...] = m_new

        def body(j, carry):
            step(j, False)
            return carry

        lax.fori_loop(0, n_full, body, 0)
        step(n_full, True)
        outs.append(acc_ref[...] / l_ref[...])
    o_ref[...] = jnp.where(lane < half, outs[0], outs[1]).astype(o_ref.dtype)


def _attention(fox, q_arr, k_arr, v_arr, aug, *, n_batch, n_pairs, t_q, t_k, tq, tk, q_off,
               q_row0, q_col0, k_col0, v_col0, out_rows, name):
    assert t_q % tq == 0 and t_k % tk == 0 and q_off % tq == 0
    assert (tq == tk and q_off % tk == 0) or tk == t_k
    nq = t_q // tq
    qw = LANES if fox else 2 * LANES
    qb0 = q_row0 // tq
    assert q_row0 % tq == 0
    in_specs = [
        pl.BlockSpec((tq, qw), lambda b, p, i: (qb0 + b * nq + i, q_col0 + p)),
        pl.BlockSpec((t_k, qw), lambda b, p, i: (b, k_col0 + p)),
        pl.BlockSpec((t_k, LANES), lambda b, p, i: (b, v_col0 + p)),
    ]
    args = [q_arr, k_arr, v_arr]
    if fox:
        qaug, kaug = aug
        ab0 = q_off // tq
        in_specs += [pl.BlockSpec((None, tq, LANES), lambda b, p, i: (b, ab0 + i, 0)),
                     pl.BlockSpec((None, t_k, LANES), lambda b, p, i: (b, 0, 0))]
        args += [qaug, kaug]
    return pl.pallas_call(
        functools.partial(_attn_kernel, fox, tq, tk, q_off),
        out_shape=jax.ShapeDtypeStruct((out_rows, n_pairs * LANES), BF16),
        grid=(n_batch, n_pairs, nq),
        in_specs=in_specs,
        out_specs=pl.BlockSpec((tq, LANES), lambda b, p, i: (b * nq + i, p)),
        scratch_shapes=[pltpu.VMEM((tq, 1), F32), pltpu.VMEM((tq, 1), F32), pltpu.VMEM((tq, LANES), F32)],
        compiler_params=_cparams(3, 48),
        name=name,
    )(*args)


def _proj_ln_kernel(nblk_p, oa_ref, ob_ref, xa_ref, xb_ref, w_ref, g_ref, b_ref, out_ref):
    is_p = pl.program_id(0) < nblk_p
    o = jnp.where(is_p, oa_ref[...], ob_ref[...])
    x = jnp.where(is_p, xa_ref[...], xb_ref[...])
    z = DEEPNORM_ALPHA * x + _dot(o, w_ref[...])
    out_ref[...] = _layernorm(z, g_ref[...], b_ref[...])


def _proj_ln(o_p, o_s, x_pair, w_o, g, b):
    n_p, d = o_p.shape
    n_s = o_s.shape[0]
    n_t = n_p + n_s
    tm = ROW_TILE
    nblk_p = n_p // tm
    if isinstance(x_pair, tuple):
        xa, xb = x_pair
        xb_spec = _rows_b(tm, d, nblk_p, 0)
    else:
        xa = xb = x_pair
        xb_spec = _rows_b(tm, d, nblk_p, nblk_p)
    return pl.pallas_call(
        functools.partial(_proj_ln_kernel, nblk_p),
        out_shape=jax.ShapeDtypeStruct((n_t, d), F32),
        grid=(n_t // tm,),
        in_specs=[_rows_a(tm, d, nblk_p), _rows_b(tm, d, nblk_p, 0),
                  _rows_a(tm, d, nblk_p), xb_spec,
                  _full((d, d)), _full((1, d)), _full((1, d))],
        out_specs=_rows(tm, d),
        compiler_params=_cparams(1, 32),
        name="proj_ln",
    )(o_p, o_s, xa, xb, w_o.astype(BF16), g.reshape(1, d), b.reshape(1, d))


def _router_kernel(x_ref, wh_ref, wl_ref, b_ref, ti_ref, tg_ref):
    xh, xl = _split2(x_ref[...])
    logits = _dot(xh, wh_ref[...]) + _dot(xl, wh_ref[...]) + _dot(xh, wl_ref[...]) + b_ref[...]
    lane = lax.broadcasted_iota(I32, logits.shape, 1).astype(F32)
    vals, idxs = [], []
    cur = logits
    for _ in range(TOP_K):
        m = jnp.max(cur, axis=1, keepdims=True)
        idx = jnp.min(jnp.where(cur == m, lane, float(LANES)), axis=1, keepdims=True)
        vals.append(m)
        idxs.append(idx)
        cur = jnp.where(lane == idx, -jnp.inf, cur)
    exps = [jnp.exp(v - vals[0]) for v in vals]
    denom = exps[0]
    for e in exps[1:]:
        denom = denom + e
    ti = jnp.zeros(logits.shape, F32)
    tg = jnp.zeros(logits.shape, F32)
    for k in range(TOP_K):
        ti = jnp.where(lane == float(k), idxs[k], ti)
        tg = jnp.where(lane == float(k), exps[k] / denom, tg)
    ti_ref[...] = ti.astype(I32)
    tg_ref[...] = tg


def _router(x, w_r, b_r):
    n_t, d = x.shape
    e = w_r.shape[-1]
    tm = ROW_TILE
    w_pad = jnp.zeros((d, LANES), F32).at[:, :e].set(w_r)
    b_pad = jnp.full((1, LANES), -jnp.inf, F32).at[0, :e].set(b_r)
    wh, wl = _split2(w_pad)
    return pl.pallas_call(
        _router_kernel,
        out_shape=(jax.ShapeDtypeStruct((n_t, LANES), I32), jax.ShapeDtypeStruct((n_t, LANES), F32)),
        grid=(n_t // tm,),
        in_specs=[_rows(tm, d), _full((d, LANES)), _full((d, LANES)), _full((1, LANES))],
        out_specs=(_rows(tm, LANES), _rows(tm, LANES)),
        compiler_params=_cparams(1, 32),
        name="router",
    )(x, wh, wl, b_pad)


def _moe_plan(top_i, n_experts):
    n = top_i.shape[0]
    blk = MOE_BLOCK
    p = n * TOP_K
    assert p % blk == 0
    flat_e = top_i.reshape(p)
    order = jnp.argsort(flat_e, stable=True).astype(I32)
    se = flat_e[order]
    counts = jnp.bincount(flat_e, length=n_experts).astype(I32)
    starts = jnp.cumsum(counts) - counts
    pcounts = (counts + blk - 1) // blk * blk
    pends = jnp.cumsum(pcounts)
    pstarts = pends - pcounts
    dest = pstarts[se] + jnp.arange(p, dtype=I32) - starts[se]
    n_blocks = p // blk + n_experts
    rows = n_blocks * blk
    block_e = jnp.minimum(jnp.searchsorted(pends, jnp.arange(n_blocks, dtype=I32) * blk, side='right'),
                          n_experts - 1).astype(I32)
    row_pair = jnp.full((rows,), -1, I32).at[dest].set(order)
    e_row = jnp.repeat(block_e, blk)
    pad_idx = jnp.arange(rows, dtype=I32) - (starts[e_row] + counts[e_row])
    valid = row_pair >= 0
    src_tok = jnp.where(valid, row_pair // TOP_K, 0)
    dst_row = jnp.where(valid, row_pair, p + pad_idx)
    return block_e, src_tok.reshape(n_blocks, 1, blk), dst_row.reshape(n_blocks, 1, blk)


def _moe_kernel(blk, nb, be_ref, src_ref, srcn_ref, dstp_ref, x_hbm, wgu_ref, bgu_ref, wdn_ref, bdn_ref,
                y_hbm, xbuf, ybuf, wgu16, wdn16, gsem, ssem):
    i = pl.program_id(0)
    slot = i % 2
    nslot = 1 - slot
    spare0 = nb * blk

    def gather_copy(tok, s, r):
        return pltpu.make_async_copy(x_hbm.at[pl.ds(tok, 1)], xbuf.at[s, pl.ds(r, 1)], gsem.at[s])

    def scatter_copy(row, s, r):
        return pltpu.make_async_copy(ybuf.at[s, pl.ds(r, 1)], y_hbm.at[pl.ds(row, 1)], ssem.at[s])

    @pl.when(i == 0)
    def _():
        ybuf[...] = jnp.zeros(ybuf.shape, F32)
        for r in range(blk):
            gather_copy(src_ref[0, r], 0, r).start()
        for r in range(blk):
            scatter_copy(spare0 + r, 0, r).start()

    @pl.when(i < nb)
    def _():
        for r in range(blk):
            gather_copy(0, slot, r).wait()
        for r in range(blk):
            scatter_copy(0, slot, r).wait()
        for r in range(blk):
            gather_copy(srcn_ref[0, r], nslot, r).start()
        for r in range(blk):
            row = jnp.where(i == 0, spare0 + blk + r, dstp_ref[0, r])
            scatter_copy(row, nslot, r).start()

        changed = jnp.logical_or(i == 0, be_ref[i] != be_ref[jnp.maximum(i - 1, 0)])

        @pl.when(changed)
        def _():
            wgu16[...] = wgu_ref[...].astype(BF16)
            wdn16[...] = wdn_ref[...].astype(BF16)

        f = wdn16.shape[0]
        x = xbuf[slot].astype(BF16)
        gu = _dot(x, wgu16[...]) + bgu_ref[...]
        g = jnp.minimum(gu[:, :f], SWIGLU_LIMIT)
        u = jnp.clip(gu[:, f:], -SWIGLU_LIMIT, SWIGLU_LIMIT)
        hid = g * jax.nn.sigmoid(SWIGLU_ALPHA * g) * (u + 1.0)
        ybuf[slot] = _dot(hid.astype(BF16), wdn16[...]) + bdn_ref[...]

    @pl.when(i == nb)
    def _():
        for r in range(blk):
            gather_copy(0, slot, r).wait()
        for r in range(blk):
            scatter_copy(0, slot, r).wait()
        for r in range(blk):
            scatter_copy(dstp_ref[0, r], nslot, r).start()
        for r in range(blk):
            scatter_copy(0, nslot, r).wait()


def _moe(x, layer, block_e, src_tok, dst_row, w_gu, b_gu, w_dn, b_dn):
    n_t, d = x.shape
    _, n_e, _, f2 = w_gu.shape
    f = f2 // 2
    blk = MOE_BLOCK
    nb = src_tok.shape[0]
    last = nb - 1
    grid_spec = pltpu.PrefetchScalarGridSpec(
        num_scalar_prefetch=1,
        grid=(nb + 1,),
        in_specs=[
            pl.BlockSpec((None, 1, blk), lambda i, be: (jnp.minimum(i, last), 0, 0), memory_space=pltpu.SMEM),
            pl.BlockSpec((None, 1, blk), lambda i, be: (jnp.minimum(i + 1, last), 0, 0), memory_space=pltpu.SMEM),
            pl.BlockSpec((None, 1, blk), lambda i, be: (jnp.clip(i - 1, 0, last), 0, 0), memory_space=pltpu.SMEM),
            pl.BlockSpec(memory_space=pl.ANY),
            pl.BlockSpec((None, None, d, f2), lambda i, be: (layer, be[jnp.minimum(i, last)], 0, 0)),
            pl.BlockSpec((None, None, 1, f2), lambda i, be: (layer, be[jnp.minimum(i, last)], 0, 0)),
            pl.BlockSpec((None, None, f, d), lambda i, be: (layer, be[jnp.minimum(i, last)], 0, 0)),
            pl.BlockSpec((None, None, 1, d), lambda i, be: (layer, be[jnp.minimum(i, last)], 0, 0)),
        ],
        out_specs=pl.BlockSpec(memory_space=pl.ANY),
        scratch_shapes=[pltpu.VMEM((2, blk, d), F32), pltpu.VMEM((2, blk, d), F32),
                        pltpu.VMEM((d, f2), BF16), pltpu.VMEM((f, d), BF16),
                        pltpu.SemaphoreType.DMA((2,)), pltpu.SemaphoreType.DMA((2,))],
    )
    return pl.pallas_call(
        functools.partial(_moe_kernel, blk, nb),
        out_shape=jax.ShapeDtypeStruct(((nb + 2) * blk, d), F32),
        grid_spec=grid_spec,
        compiler_params=_cparams(1, 56),
        name="moe_experts",
    )(block_e, src_tok, src_tok, dst_row, x, w_gu, b_gu.reshape(b_gu.shape[0], n_e, 1, f2),
      w_dn, b_dn.reshape(b_dn.shape[0], n_e, 1, d))


def _combine_ln_kernel(nblk_p, split, y_ref, tg_ref, x_ref, g_ref, b_ref, *out_refs):
    d = x_ref.shape[-1]
    tg = tg_ref[...]
    y = tg[:, 0:1] * y_ref[:, 0:d]
    for k in range(1, TOP_K):
        y = y + tg[:, k:k + 1] * y_ref[:, k * d:(k + 1) * d]
    out = _layernorm(DEEPNORM_ALPHA * x_ref[...] + y, g_ref[...], b_ref[...])
    if split:
        is_p = pl.program_id(0) < nblk_p

        @pl.when(is_p)
        def _():
            out_refs[0][...] = out

        @pl.when(jnp.logical_not(is_p))
        def _():
            out_refs[1][...] = out
    else:
        out_refs[0][...] = out


def _combine_ln(y_rows, tg, x, g, b, n_p, split):
    n_t, d = x.shape
    tm = ROW_TILE
    nblk_p = n_p // tm
    y4 = y_rows.reshape(y_rows.shape[0] // TOP_K, TOP_K * d)
    if split:
        out_shape = (jax.ShapeDtypeStruct((n_p, d), F32), jax.ShapeDtypeStruct((n_t - n_p, d), F32))
        out_specs = (_rows_a(tm, d, nblk_p), _rows_b(tm, d, nblk_p, 0))
    else:
        out_shape = jax.ShapeDtypeStruct((n_t, d), F32)
        out_specs = _rows(tm, d)
    return pl.pallas_call(
        functools.partial(_combine_ln_kernel, nblk_p, split),
        out_shape=out_shape,
        grid=(n_t // tm,),
        in_specs=[_rows(tm, TOP_K * d), _rows(tm, LANES), _rows(tm, d), _full((1, d)), _full((1, d))],
        out_specs=out_specs,
        compiler_params=_cparams(1, 32),
        name="combine_ln",
    )(y4, tg, x, g.reshape(1, d), b.reshape(1, d))


def _moe_layer(x, layer, n_p, split, ln_g, ln_b, w_r, b_r, w_gu, b_gu, w_dn, b_dn):
    ti, tg = _router(x, w_r[layer], b_r[layer])
    block_e, src_tok, dst_row = _moe_plan(ti[:, :TOP_K], w_r.shape[-1])
    y_rows = _moe(x, layer, block_e, src_tok, dst_row, w_gu, b_gu, w_dn, b_dn)
    return _combine_ln(y_rows, tg, x, ln_g, ln_b, n_p, split)


def _mla_proj_kernel(nblk_p, q_lora, kv_lora, n_heads, x_ref, w1_ref, gq_ref, w2a_ref, w2b_ref, gkv_ref,
                     qc_ref, qs_ref, kc_ref, ks_ref,
                     q_ref, ckvp_ref, ckvs_ref, kpep_ref, kpes_ref, ckv16_ref, kpe16_ref):
    is_p = pl.program_id(0) < nblk_p
    x = x_ref[...].astype(BF16)
    y1 = _dot(x, w1_ref[...])
    a = _rmsnorm(y1[:, :q_lora], gq_ref[...]).astype(BF16)
    qa = _dot(a, w2a_ref[...])
    qb = _dot(a, w2b_ref[...])
    qc = qc_ref[...]
    qs = qs_ref[...]
    for h in range(n_heads):
        cols = slice(h * LANES, (h + 1) * LANES)
        q_ref[:, cols] = (qa[:, cols] * qc + qb[:, cols] * qs).astype(BF16)
    ckv = _rmsnorm(y1[:, q_lora:q_lora + kv_lora], gkv_ref[...])
    c0 = q_lora + kv_lora
    kpe = y1[:, c0:c0 + LANES] * kc_ref[...] + y1[:, c0 + LANES:c0 + 2 * LANES] * ks_ref[...]
    ckv16_ref[...] = ckv.astype(BF16)
    kpe16_ref[...] = kpe.astype(BF16)
    rope = kpep_ref.shape[-1]

    @pl.when(is_p)
    def _():
        ckvp_ref[...] = ckv
        kpep_ref[...] = kpe[:, :rope]

    @pl.when(jnp.logical_not(is_p))
    def _():
        ckvs_ref[...] = ckv
        kpes_ref[...] = kpe[:, :rope]


def _rot_cols(w):
    half = w.shape[-1] // 2
    return jnp.concatenate([-w[..., half:], w[..., :half]], axis=-1)


def _mla_proj(x, n_p, t_p, t_s, past, w_dq, g_q, w_uq, w_dkv, g_kv):
    n_t, d = x.shape
    n_s = n_t - n_p
    tm = ROW_TILE
    nblk_p = n_p // tm
    q_lora = w_dq.shape[-1]
    kv_lora = g_kv.shape[-1]
    hd = MLA_NOPE + MLA_ROPE
    n_heads = w_uq.shape[-1] // hd
    assert q_lora % LANES == 0 and kv_lora % LANES == 0 and t_p % tm == 0 and tm % t_s == 0

    w_pe = w_dkv[:, kv_lora:]
    pad = jnp.zeros((d, LANES - MLA_ROPE), F32)
    w1 = jnp.concatenate([w_dq, w_dkv[:, :kv_lora], w_pe, pad, _rot_cols(w_pe), pad], axis=1).astype(BF16)

    w3 = w_uq.reshape(q_lora, n_heads, hd)
    zq = jnp.zeros((q_lora, n_heads, LANES - hd), F32)
    w2a = jnp.concatenate([w3, zq], axis=2).reshape(q_lora, n_heads * LANES).astype(BF16)
    w2b = jnp.concatenate([jnp.zeros((q_lora, n_heads, MLA_NOPE), F32), _rot_cols(w3[:, :, MLA_NOPE:]), zq],
                          axis=2).reshape(q_lora, n_heads * LANES).astype(BF16)

    half = MLA_ROPE // 2
    inv_freq = ROPE_THETA ** (-jnp.arange(half, dtype=F32) / half)
    pos = jnp.concatenate([jnp.arange(t_p), past + (jnp.arange(tm) % t_s)]).astype(F32)
    ang = pos[:, None] * inv_freq[None, :]
    cos2 = jnp.concatenate([jnp.cos(ang), jnp.cos(ang)], axis=1)
    sin2 = jnp.concatenate([jnp.sin(ang), jnp.sin(ang)], axis=1)
    rows = pos.shape[0]
    scale = hd ** -0.5
    z32 = jnp.zeros((rows, LANES - hd), F32)
    q_cos = jnp.concatenate([jnp.full((rows, MLA_NOPE), scale, F32), scale * cos2, z32], axis=1)
    q_sin = jnp.concatenate([jnp.zeros((rows, MLA_NOPE), F32), scale * sin2, z32], axis=1)
    z96 = jnp.zeros((rows, LANES - MLA_ROPE), F32)
    k_cos = jnp.concatenate([cos2, z96], axis=1)
    k_sin = jnp.concatenate([sin2, z96], axis=1)
    tpb = t_p // tm
    tbl = pl.BlockSpec((tm, LANES), lambda i: (jnp.where(i < nblk_p, i % tpb, tpb), 0))

    w1w = w1.shape[1]
    return pl.pallas_call(
        functools.partial(_mla_proj_kernel, nblk_p, q_lora, kv_lora, n_heads),
        out_shape=(jax.ShapeDtypeStruct((n_t, n_heads * LANES), BF16),
                   jax.ShapeDtypeStruct((n_p, kv_lora), F32), jax.ShapeDtypeStruct((n_s, kv_lora), F32),
                   jax.ShapeDtypeStruct((n_p, MLA_ROPE), F32), jax.ShapeDtypeStruct((n_s, MLA_ROPE), F32),
                   jax.ShapeDtypeStruct((n_t, kv_lora), BF16), jax.ShapeDtypeStruct((n_t, LANES), BF16)),
        grid=(n_t // tm,),
        in_specs=[_rows(tm, d), _full((d, w1w)), _full((1, q_lora)),
                  _full((q_lora, n_heads * LANES)), _full((q_lora, n_heads * LANES)), _full((1, kv_lora)),
                  tbl, tbl, tbl, tbl],
        out_specs=(_rows(tm, n_heads * LANES),
                   _rows_a(tm, kv_lora, nblk_p), _rows_b(tm, kv_lora, nblk_p, 0),
                   _rows_a(tm, MLA_ROPE, nblk_p), _rows_b(tm, MLA_ROPE, nblk_p, 0),
                   _rows(tm, kv_lora), _rows(tm, LANES)),
        compiler_params=_cparams(1, 48),
        name="mla_proj",
    )(x, w1, g_q.reshape(1, q_lora), w2a, w2b, g_kv.reshape(1, kv_lora), q_cos, q_sin, k_cos, k_sin)


def _mla_expand_kernel(ckv_ref, kpe_ref, wk_ref, pk_ref, wv_ref, k_ref, v_ref):
    ckv = ckv_ref[...]
    k_ref[...] = (_dot(ckv, wk_ref[...]) + _dot(kpe_ref[...], pk_ref[...])).astype(BF16)
    v_ref[...] = _dot(ckv, wv_ref[...]).astype(BF16)


def _mla_expand(ckv16, kpe16, n_rows, w_ukv):
    kv_lora = ckv16.shape[-1]
    hd = MLA_NOPE + MLA_V
    n_heads = w_ukv.shape[-1] // hd
    tm = next(t for t in (512, 256, LANES) if n_rows % t == 0)
    w3 = w_ukv.reshape(kv_lora, n_heads, hd)
    wk = jnp.concatenate([w3[:, :, :MLA_NOPE], jnp.zeros((kv_lora, n_heads, LANES - MLA_NOPE), F32)],
                         axis=2).reshape(kv_lora, n_heads * LANES).astype(BF16)
    wv = w3[:, :, MLA_NOPE:].reshape(kv_lora, n_heads * MLA_V).astype(BF16)
    pk = np.zeros((LANES, n_heads, LANES), np.float32)
    for j in range(MLA_ROPE):
        pk[j, :, MLA_NOPE + j] = 1.0
    pk = jnp.asarray(pk.reshape(LANES, n_heads * LANES), BF16)
    return pl.pallas_call(
        _mla_expand_kernel,
        out_shape=(jax.ShapeDtypeStruct((n_rows, n_heads * LANES), BF16),
                   jax.ShapeDtypeStruct((n_rows, n_heads * MLA_V), BF16)),
        grid=(n_rows // tm,),
        in_specs=[_rows(tm, kv_lora), _rows(tm, LANES), _full((kv_lora, n_heads * LANES)),
                  _full((LANES, n_heads * LANES)), _full((kv_lora, n_heads * MLA_V))],
        out_specs=(_rows(tm, n_heads * LANES), _rows(tm, n_heads * MLA_V)),
        compiler_params=_cparams(1, 32),
        name="mla_expand",
    )(ckv16, kpe16, wk, pk, wv)


def _pad_keys(cache, new, t_pad):
    b, t_c, w = cache.shape
    t_n = new.shape[1]
    parts = [cache.astype(BF16), new.astype(BF16)]
    if t_pad > t_c + t_n:
        parts.append(jnp.zeros((b, t_pad - t_c - t_n, w), BF16))
    return jnp.concatenate(parts, axis=1).reshape(b * t_pad, w)


def kernel(x_prompt, x_sample, cache_fox_k, cache_fox_v, cache_fox_logf, cache_mla_ckv, cache_mla_kpe, fox_w_qkv, fox_w_f, fox_b_f, fox_w_o, mla_w_dq, mla_g_q, mla_w_uq, mla_w_dkv, mla_g_kv, mla_w_ukv, mla_w_o, ln_g, ln_b, moe_w_router, moe_b_router, moe_w_gu, moe_b_gu, moe_w_down, moe_b_down):
    b_p, t_p, d = x_prompt.shape
    b_s, t_s, _ = x_sample.shape
    past = cache_fox_k.shape[2]
    n_p, n_s = b_p * t_p, b_s * t_s
    assert fox_w_qkv.shape[0] == 1 and mla_w_dq.shape[0] == 1 and ln_g.shape[0] == DEPTH
    assert n_p % ROW_TILE == 0 and n_s % ROW_TILE == 0 and t_p % ATTN_TILE == 0
    xp = x_prompt.reshape(n_p, d)
    xs = x_sample.reshape(n_s, d)
    moe_w = (moe_w_router, moe_b_router, moe_w_gu, moe_b_gu, moe_w_down, moe_b_down)
    t_all = -(-(past + t_s) // LANES) * LANES

    n_fh = fox_w_f.shape[-1]
    fd = d // n_fh
    assert 2 * fd == LANES
    qkv16, k_p, k_s, v_p, v_s, logf = _fox_proj(xp, xs, fox_w_qkv[0], fox_w_f[0], fox_b_f[0])
    logf_p = logf[:n_p].reshape(b_p, t_p, n_fh)
    logf_s = logf[n_p:].reshape(b_s, t_s, n_fh)
    qaug_p, kaug_p = _fox_aug(logf_p)
    logf_all = jnp.concatenate([cache_fox_logf[0].astype(F32), logf_s,
                                jnp.zeros((b_s, t_all - past - t_s, n_fh), F32)], axis=1)
    qaug_s, kaug_s = _fox_aug(logf_all)
    n_pairs = n_fh // 2
    o_p = _attention(True, qkv16, qkv16, qkv16, (qaug_p, kaug_p), n_batch=b_p, n_pairs=n_pairs,
                     t_q=t_p, t_k=t_p, tq=ATTN_TILE, tk=ATTN_TILE, q_off=0, q_row0=0,
                     q_col0=0, k_col0=n_pairs, v_col0=2 * n_pairs, out_rows=n_p, name="fox_attn_prompt")
    k_all = _pad_keys(cache_fox_k[0].reshape(b_s, past, d), k_s.reshape(b_s, t_s, d), t_all)
    v_all = _pad_keys(cache_fox_v[0].reshape(b_s, past, d), v_s.reshape(b_s, t_s, d), t_all)
    o_s = _attention(True, qkv16, k_all, v_all, (qaug_s, kaug_s), n_batch=b_s, n_pairs=n_pairs,
                     t_q=t_s, t_k=t_all, tq=t_s, tk=t_all, q_off=past, q_row0=n_p,
                     q_col0=0, k_col0=0, v_col0=0, out_rows=n_s, name="fox_attn_sample")
    x1 = _proj_ln(o_p, o_s, (xp, xs), fox_w_o[0], ln_g[0, 0], ln_b[0, 0])
    x2 = _moe_layer(x1, 0, n_p, False, ln_g[0, 1], ln_b[0, 1], *moe_w)

    q16, ckv_p, ckv_s, kpe_p, kpe_s, ckv16, kpe16 = _mla_proj(
        x2, n_p, t_p, t_s, past, mla_w_dq[0], mla_g_q[0], mla_w_uq[0], mla_w_dkv[0], mla_g_kv[0])
    kv_lora = ckv_p.shape[-1]
    n_mh = mla_w_o.shape[1] // MLA_V
    m_pairs = n_mh // 2
    k16_p, v16_p = _mla_expand(ckv16, kpe16, n_p, mla_w_ukv[0])
    o_p = _attention(False, q16, k16_p, v16_p, None, n_batch=b_p, n_pairs=m_pairs,
                     t_q=t_p, t_k=t_p, tq=ATTN_TILE, tk=ATTN_TILE, q_off=0, q_row0=0,
                     q_col0=0, k_col0=0, v_col0=0, out_rows=n_p, name="mla_attn_prompt")
    ckv_all = _pad_keys(cache_mla_ckv[0], ckv16[n_p:].reshape(b_s, t_s, kv_lora), t_all)
    kpe_cache = jnp.pad(cache_mla_kpe[0], ((0, 0), (0, 0), (0, LANES - MLA_ROPE)))
    kpe_all = _pad_keys(kpe_cache, kpe16[n_p:].reshape(b_s, t_s, LANES), t_all)
    k16_s, v16_s = _mla_expand(ckv_all, kpe_all, b_s * t_all, mla_w_ukv[0])
    o_s = _attention(False, q16, k16_s, v16_s, None, n_batch=b_s, n_pairs=m_pairs,
                     t_q=t_s, t_k=t_all, tq=t_s, tk=t_all, q_off=past, q_row0=n_p,
                     q_col0=0, k_col0=0, v_col0=0, out_rows=n_s, name="mla_attn_sample")
    x3 = _proj_ln(o_p, o_s, x2, mla_w_o[0], ln_g[1, 0], ln_b[1, 0])
    y_p, y_s = _moe_layer(x3, 1, n_p, True, ln_g[1, 1], ln_b[1, 1], *moe_w)

    return (y_p.reshape(b_p, t_p, d), y_s.reshape(b_s, t_s, d),
            k_p.reshape(1, b_p, t_p, n_fh, fd), v_p.reshape(1, b_p, t_p, n_fh, fd),
            logf_p.reshape(1, b_p, t_p, n_fh),
            ckv_p.reshape(1, b_p, t_p, kv_lora), kpe_p.reshape(1, b_p, t_p, MLA_ROPE),
            k_s.reshape(1, b_s, t_s, n_fh, fd), v_s.reshape(1, b_s, t_s, n_fh, fd),
            logf_s.reshape(1, b_s, t_s, n_fh),
            ckv_s.reshape(1, b_s, t_s, kv_lora), kpe_s.reshape(1, b_s, t_s, MLA_ROPE))
```

```python
import functools

import numpy as np
import jax
import jax.numpy as jnp
from jax import lax
from jax.experimental import pallas as pl
from jax.experimental.pallas import tpu as pltpu

F32 = jnp.float32
BF16 = jnp.bfloat16
I32 = jnp.int32

DEPTH = 2
CHUNK = 64
CHUNK_LOG2 = 6
assert 1 << CHUNK_LOG2 == CHUNK
MLA_NOPE = 64
MLA_ROPE = 32
MLA_V = 64
ROPE_THETA = 10000.0
TOP_K = 4
SWIGLU_LIMIT = 7.0
SWIGLU_ALPHA = 1.702
DEEPNORM_ALPHA = (2.0 * DEPTH) ** 0.25
LN_EPS = 1e-5
RMS_EPS = 1e-6
NEG_INF = -1e30

LANES = 128
SUBLANES = 8
AUG_COLS = 6
MIB = 1 << 20

ROW_TILE = 256
MOE_BLOCK = 256
ATTN_TILE = 2048
ATTN_SUB = 256


def _cparams(n_axes, vmem_mib):
    return pltpu.CompilerParams(dimension_semantics=("arbitrary",) * n_axes,
                                vmem_limit_bytes=vmem_mib * MIB)


def _full(shape):
    nd = len(shape)
    return pl.BlockSpec(shape, lambda *_: (0,) * nd)


def _rows(tm, width):
    return pl.BlockSpec((tm, width), lambda i: (i, 0))


def _rows_a(tm, width, nblk_a):
    return pl.BlockSpec((tm, width), lambda i: (jnp.minimum(i, nblk_a - 1), 0))


def _rows_b(tm, width, nblk_a, base):
    return pl.BlockSpec((tm, width), lambda i: (base + jnp.maximum(i - nblk_a, 0), 0))


def _split2(x):
    hi = x.astype(BF16)
    lo = (x - hi.astype(F32)).astype(BF16)
    return hi, lo


def _split3(x):
    hi = x.astype(BF16)
    r = x - hi.astype(F32)
    mid = r.astype(BF16)
    lo = (r - mid.astype(F32)).astype(BF16)
    return hi, mid, lo


def _dot(a, b):
    return jnp.dot(a, b, preferred_element_type=F32)


def _layernorm(z, g, b):
    mu = jnp.mean(z, axis=-1, keepdims=True)
    zc = z - mu
    var = jnp.mean(zc * zc, axis=-1, keepdims=True)
    return zc * lax.rsqrt(var + LN_EPS) * g + b


def _rmsnorm(x, g):
    return x * lax.rsqrt(jnp.mean(x * x, axis=-1, keepdims=True) + RMS_EPS) * g


def _fox_proj_kernel(nblk_p, xa_ref, xb_ref, wqkv_ref, wfh_ref, wfl_ref, bf_ref,
                     qkv_ref, kp_ref, ks_ref, vp_ref, vs_ref, logf_ref):
    i = pl.program_id(0)
    is_p = i < nblk_p
    x = jnp.where(is_p, xa_ref[...], xb_ref[...])
    xh, xl = _split2(x)
    qkv = _dot(xh, wqkv_ref[...])
    qkv_ref[...] = qkv.astype(BF16)
    d = kp_ref.shape[-1]
    k = qkv[:, d:2 * d]
    v = qkv[:, 2 * d:3 * d]

    @pl.when(is_p)
    def _():
        kp_ref[...] = k
        vp_ref[...] = v

    @pl.when(jnp.logical_not(is_p))
    def _():
        ks_ref[...] = k
        vs_ref[...] = v

    z = _dot(xh, wfh_ref[...]) + _dot(xl, wfh_ref[...]) + _dot(xh, wfl_ref[...]) + bf_ref[...]
    logf_ref[...] = -(jnp.maximum(-z, 0.0) + jnp.log1p(jnp.exp(-jnp.abs(z))))


def _fox_proj(xp, xs, w_qkv, w_f, b_f):
    n_p, d = xp.shape
    n_s = xs.shape[0]
    n_t = n_p + n_s
    h = w_f.shape[-1]
    tm = ROW_TILE
    nblk_p = n_p // tm
    wfh, wfl = _split2(w_f)
    return pl.pallas_call(
        functools.partial(_fox_proj_kernel, nblk_p),
        out_shape=(jax.ShapeDtypeStruct((n_t, 3 * d), BF16),
                   jax.ShapeDtypeStruct((n_p, d), F32), jax.ShapeDtypeStruct((n_s, d), F32),
                   jax.ShapeDtypeStruct((n_p, d), F32), jax.ShapeDtypeStruct((n_s, d), F32),
                   jax.ShapeDtypeStruct((n_t, h), F32)),
        grid=(n_t // tm,),
        in_specs=[_rows_a(tm, d, nblk_p), _rows_b(tm, d, nblk_p, 0),
                  _full((d, 3 * d)), _full((d, h)), _full((d, h)), _full((1, h))],
        out_specs=(_rows(tm, 3 * d),
                   _rows_a(tm, d, nblk_p), _rows_b(tm, d, nblk_p, 0),
                   _rows_a(tm, d, nblk_p), _rows_b(tm, d, nblk_p, 0),
                   _rows(tm, h)),
        compiler_params=_cparams(1, 48),
        name="fox_proj",
    )(xp, xs, w_qkv.astype(BF16), wfh, wfl, b_f.reshape(1, h))


def _fox_aug_kernel(chunk, logf_ref, place_ref, ones_ref, qaug_ref, kaug_ref):
    t, h = logf_ref.shape
    r = lax.broadcasted_iota(I32, (chunk, chunk), 0)
    c = lax.broadcasted_iota(I32, (chunk, chunk), 1)
    tri = jnp.where(c <= r, 1.0, 0.0).astype(BF16)
    carry = jnp.zeros((1, h), F32)
    for j in range(t // chunk):
        rows = pl.ds(j * chunk, chunk)
        hi, mid, lo = _split3(logf_ref[rows, :])
        cs = _dot(tri, hi) + _dot(tri, mid) + _dot(tri, lo) + carry
        carry = cs[chunk - 1:chunk, :]
        chi, cmid, clo = _split3(cs)
        qa = _dot(chi, place_ref[0]) + _dot(cmid, place_ref[1]) + _dot(clo, place_ref[2]) + ones_ref[0]
        ka = ones_ref[1] - (_dot(chi, place_ref[3]) + _dot(cmid, place_ref[4]) + _dot(clo, place_ref[5]))
        qaug_ref[rows, :] = qa.astype(BF16)
        kaug_ref[rows, :] = ka.astype(BF16)


def _fox_aug(logf3):
    b, t, h = logf3.shape
    assert h * AUG_COLS <= LANES
    chunk = 512 if t % 512 == 0 else LANES
    assert t % chunk == 0
    place = np.zeros((AUG_COLS, h, LANES), np.float32)
    ones = np.zeros((2, 1, LANES), np.float32)
    for hh in range(h):
        for j in range(AUG_COLS):
            place[j, hh, AUG_COLS * hh + j] = 1.0
        ones[0, 0, AUG_COLS * hh + 3:AUG_COLS * hh + 6] = 1.0
        ones[1, 0, AUG_COLS * hh:AUG_COLS * hh + 3] = 1.0
    return pl.pallas_call(
        functools.partial(_fox_aug_kernel, chunk),
        out_shape=(jax.ShapeDtypeStruct((b, t, LANES), BF16), jax.ShapeDtypeStruct((b, t, LANES), BF16)),
        grid=(b,),
        in_specs=[pl.BlockSpec((None, t, h), lambda i: (i, 0, 0)),
                  _full((AUG_COLS, h, LANES)), _full((2, 1, LANES))],
        out_specs=(pl.BlockSpec((None, t, LANES), lambda i: (i, 0, 0)),
                   pl.BlockSpec((None, t, LANES), lambda i: (i, 0, 0))),
        compiler_params=_cparams(1, 32),
        name="fox_aug",
    )(logf3, jnp.asarray(place, BF16), jnp.asarray(ones, F32))


def _attn_kernel(fox, tq, tk, q_off, nq, *refs):
    if fox:
        q_ref, k_ref, v_ref, qaug_ref, kaug_ref, o_ref = refs
    else:
        q_ref, k_ref, v_ref, o_ref = refs
    pair = pl.program_id(1)
    iq = pl.program_id(2)
    half = LANES // 2
    sub = min(tq, ATTN_SUB)
    lane = lax.broadcasted_iota(I32, (1, LANES), 1)
    lo_half = lane < half
    qops = []
    for hh in range(2):
        if fox:
            head = 2 * pair + hh
            amask = (lane >= AUG_COLS * head) & (lane < AUG_COLS * head + AUG_COLS)
            scale = half ** -0.5
            qops.append(jnp.concatenate(
                [jnp.where(lo_half == (hh == 0), q_ref[...], 0.0).astype(BF16) * jnp.asarray(scale, BF16),
                 jnp.where(amask, qaug_ref[...], 0.0).astype(BF16)], axis=1))
        else:
            qops.append(q_ref[:, hh * LANES:(hh + 1) * LANES])

    ones_lo = jnp.where(lo_half, 1.0, 0.0).astype(BF16)
    ones_hi = jnp.where(lo_half, 0.0, 1.0).astype(BF16)

    def attend(iq_s):
        q_first = q_off + iq_s * tq
        kf = (q_first // tk) * tk
        for r in range(tq // sub):
            rsl = slice(r * sub, (r + 1) * sub)
            nd = (r + 1) * sub if tq == tk else tk
            nk = kf + nd
            qpos = q_first + r * sub + lax.broadcasted_iota(I32, (sub, nd), 0)
            kpos = kf + lax.broadcasted_iota(I32, (sub, nd), 1)
            if fox:
                allowed = kpos <= qpos
            else:
                allowed = (lax.shift_right_logical(kpos, CHUNK_LOG2)
                           <= lax.shift_right_logical(qpos, CHUNK_LOG2))
            ps = []
            for hh in range(2):
                if fox:
                    kop = jnp.concatenate([k_ref[:nk, :], kaug_ref[:nk, :]], axis=1)
                else:
                    kop = k_ref[:nk, hh * LANES:(hh + 1) * LANES]
                s = lax.dot_general(qops[hh][rsl], kop, (((1,), (1,)), ((), ())), preferred_element_type=F32)
                s_last = jnp.where(allowed, s[:, kf:], NEG_INF)
                s = s_last if kf == 0 else jnp.concatenate([s[:, :kf], s_last], axis=1)
                m = jnp.max(s, axis=1, keepdims=True)
                ps.append(jnp.exp(s - m).astype(BF16))
            v = v_ref[:nk, :]
            vop = jnp.concatenate(
                [jnp.concatenate([jnp.where(lo_half, v, 0.0).astype(BF16), jnp.broadcast_to(ones_lo, v.shape)], axis=1),
                 jnp.concatenate([jnp.where(lo_half, 0.0, v).astype(BF16), jnp.broadcast_to(ones_hi, v.shape)], axis=1)],
                axis=0)
            acc = _dot(jnp.concatenate(ps, axis=1), vop)
            o_ref[rsl, :] = (acc[:, :LANES] / acc[:, LANES:]).astype(o_ref.dtype)

    for iq_s in range(nq):
        pl.when(iq == iq_s)(functools.partial(attend, iq_s))


def _attention(fox, q_arr, k_arr, v_arr, aug, *, n_batch, n_pairs, t_q, t_k, tq, tk, q_off,
               q_row0, q_col0, k_col0, v_col0, out_rows, name):
    assert t_q % tq == 0 and t_k % tk == 0 and q_off % tq == 0
    assert (tq == tk and q_off % tk == 0) or tk == t_k
    nq = t_q // tq
    qw = LANES if fox else 2 * LANES
    qb0 = q_row0 // tq
    assert q_row0 % tq == 0
    in_specs = [
        pl.BlockSpec((tq, qw), lambda b, p, i: (qb0 + b * nq + i, q_col0 + p)),
        pl.BlockSpec((t_k, qw), lambda b, p, i: (b, k_col0 + p)),
        pl.BlockSpec((t_k, LANES), lambda b, p, i: (b, v_col0 + p)),
    ]
    args = [q_arr, k_arr, v_arr]
    if fox:
        qaug, kaug = aug
        ab0 = q_off // tq
        in_specs += [pl.BlockSpec((None, tq, LANES), lambda b, p, i: (b, ab0 + i, 0)),
                     pl.BlockSpec((None, t_k, LANES), lambda b, p, i: (b, 0, 0))]
        args += [qaug, kaug]
    return pl.pallas_call(
        functools.partial(_attn_kernel, fox, tq, tk, q_off, nq),
        out_shape=jax.ShapeDtypeStruct((out_rows, n_pairs * LANES), BF16),
        grid=(n_batch, n_pairs, nq),
        in_specs=in_specs,
        out_specs=pl.BlockSpec((tq, LANES), lambda b, p, i: (b * nq + i, p)),
        compiler_params=_cparams(3, 48),
        name=name,
    )(*args)


def _to_row_tiles(ref, val):
    rows = val.shape[0]
    for c in range(SUBLANES):
        ref[pl.ds(c, rows, stride=SUBLANES), :] = val[:, c * LANES:(c + 1) * LANES]


def _from_row_tiles(ref):
    rows = ref.shape[0] // SUBLANES
    return jnp.concatenate([ref[pl.ds(c, rows, stride=SUBLANES), :] for c in range(SUBLANES)], axis=1)


def _proj_ln_kernel(nblk_p, oa_ref, ob_ref, xa_ref, xb_ref, w_ref, g_ref, b_ref, out_ref, out3_ref):
    is_p = pl.program_id(0) < nblk_p
    o = jnp.where(is_p, oa_ref[...], ob_ref[...])
    x = jnp.where(is_p, xa_ref[...], xb_ref[...])
    z = DEEPNORM_ALPHA * x + _dot(o, w_ref[...])
    out = _layernorm(z, g_ref[...], b_ref[...])
    out_ref[...] = out
    _to_row_tiles(out3_ref, out)


def _proj_ln(o_p, o_s, x_pair, w_o, g, b):
    n_p, d = o_p.shape
    n_s = o_s.shape[0]
    n_t = n_p + n_s
    tm = ROW_TILE
    nblk_p = n_p // tm
    if isinstance(x_pair, tuple):
        xa, xb = x_pair
        xb_spec = _rows_b(tm, d, nblk_p, 0)
    else:
        xa = xb = x_pair
        xb_spec = _rows_b(tm, d, nblk_p, nblk_p)
    return pl.pallas_call(
        functools.partial(_proj_ln_kernel, nblk_p),
        out_shape=(jax.ShapeDtypeStruct((n_t, d), F32),
                   jax.ShapeDtypeStruct((n_t * SUBLANES, LANES), F32)),
        grid=(n_t // tm,),
        in_specs=[_rows_a(tm, d, nblk_p), _rows_b(tm, d, nblk_p, 0),
                  _rows_a(tm, d, nblk_p), xb_spec,
                  _full((d, d)), _full((1, d)), _full((1, d))],
        out_specs=(_rows(tm, d), _rows(tm * SUBLANES, LANES)),
        compiler_params=_cparams(1, 32),
        name="proj_ln",
    )(o_p, o_s, xa, xb, w_o.astype(BF16), g.reshape(1, d), b.reshape(1, d))


def _router_kernel(x_ref, wh_ref, wl_ref, b_ref, ti_ref, tg_ref, cnt_ref):
    xh, xl = _split2(x_ref[...])
    logits = _dot(xh, wh_ref[...]) + _dot(xl, wh_ref[...]) + _dot(xh, wl_ref[...]) + b_ref[...]
    lane = lax.broadcasted_iota(I32, logits.shape, 1).astype(F32)
    vals, idxs = [], []
    cur = logits
    for _ in range(TOP_K):
        m = jnp.max(cur, axis=1, keepdims=True)
        idx = jnp.min(jnp.where(cur == m, lane, float(LANES)), axis=1, keepdims=True)
        vals.append(m)
        idxs.append(idx)
        cur = jnp.where(lane == idx, -jnp.inf, cur)
    exps = [jnp.exp(v - vals[0]) for v in vals]
    denom = exps[0]
    for e in exps[1:]:
        denom = denom + e
    ti = jnp.zeros(logits.shape, F32)
    tg = jnp.zeros(logits.shape, F32)
    for k in range(TOP_K):
        ti = jnp.where(lane == float(k), idxs[k], ti)
        tg = jnp.where(lane == float(k), exps[k] / denom, tg)
    ti_ref[...] = ti.astype(I32)
    tg_ref[...] = tg

    hot = jnp.zeros(logits.shape, F32)
    for k in range(TOP_K):
        hot = hot + jnp.where(lane == idxs[k], 1.0, 0.0)

    @pl.when(pl.program_id(0) == 0)
    def _():
        cnt_ref[...] = jnp.zeros(cnt_ref.shape, F32)

    cnt_ref[...] += jnp.sum(hot, axis=0, keepdims=True)


def _router(x, w_r, b_r):
    n_t, d = x.shape
    e = w_r.shape[-1]
    tm = ROW_TILE
    w_pad = jnp.zeros((d, LANES), F32).at[:, :e].set(w_r)
    b_pad = jnp.full((1, LANES), -jnp.inf, F32).at[0, :e].set(b_r)
    wh, wl = _split2(w_pad)
    return pl.pallas_call(
        _router_kernel,
        out_shape=(jax.ShapeDtypeStruct((n_t, LANES), I32), jax.ShapeDtypeStruct((n_t, LANES), F32),
                   jax.ShapeDtypeStruct((8, LANES), F32)),
        grid=(n_t // tm,),
        in_specs=[_rows(tm, d), _full((d, LANES)), _full((d, LANES)), _full((1, LANES))],
        out_specs=(_rows(tm, LANES), _rows(tm, LANES), _full((8, LANES))),
        compiler_params=_cparams(1, 32),
        name="router",
    )(x, wh, wl, b_pad)


def _moe_plan(top_i, counts):
    n = top_i.shape[0]
    n_experts = counts.shape[0]
    blk = MOE_BLOCK
    p = n * TOP_K
    assert p % blk == 0
    order = jnp.argsort(top_i.reshape(p), stable=True).astype(I32)
    starts = jnp.cumsum(counts) - counts
    nblk_e = (counts + blk - 1) // blk
    bends = jnp.cumsum(nblk_e)
    n_blocks = p // blk + n_experts
    blk_ids = jnp.arange(n_blocks, dtype=I32)
    block_e = jnp.minimum(jnp.sum((bends[None, :] <= blk_ids[:, None]).astype(I32), axis=1), n_experts - 1)
    b_first = (blk_ids - (bends - nblk_e)[block_e]) * blk
    b_cnt = counts[block_e]
    b_start = starts[block_e]
    r_in = jnp.arange(blk, dtype=I32)[None, :]
    off = b_first[:, None] + r_in
    valid = off < b_cnt[:, None]
    pair = order[jnp.clip(b_start[:, None] + off, 0, p - 1)]
    row = blk_ids[:, None] * blk + r_in
    pad_idx = row - (b_start + b_cnt)[:, None]
    src_tok = jnp.where(valid, pair // TOP_K, 0)
    dst_row = jnp.where(valid, (pair % TOP_K) * n + pair // TOP_K, p + pad_idx)
    return block_e, src_tok.reshape(n_blocks, 1, blk), dst_row.reshape(n_blocks, 1, blk)


def _moe_kernel(blk, nb, be_ref, src_ref, srcn_ref, dstp_ref, x_hbm, wgu_ref, bgu_ref, wdn_ref, bdn_ref,
                y_hbm, xb0, xb1, yb0, yb1, wgu16, wdn16, gsem, ssem):
    i = pl.program_id(0)
    spare0 = nb * blk

    xbuf = (xb0, xb1)
    ybuf = (yb0, yb1)

    def gather_copy(tok, s, r):
        return pltpu.make_async_copy(x_hbm.at[tok], xbuf[s].at[pl.ds(r * SUBLANES, SUBLANES)], gsem.at[s])

    def scatter_copy(row, s, r):
        return pltpu.make_async_copy(ybuf[s].at[pl.ds(r * SUBLANES, SUBLANES)], y_hbm.at[row], ssem.at[s])

    @pl.when(i == 0)
    def _():
        yb0[...] = jnp.zeros(yb0.shape, F32)
        yb1[...] = jnp.zeros(yb1.shape, F32)
        for r in range(blk):
            gather_copy(src_ref[0, r], 0, r).start()
        for r in range(blk):
            scatter_copy(spare0 + r, 0, r).start()

    changed = jnp.logical_or(i == 0, be_ref[jnp.minimum(i, nb - 1)] != be_ref[jnp.clip(i - 1, 0, nb - 1)])

    @pl.when(jnp.logical_and(i < nb, changed))
    def _():
        wgu16[...] = wgu_ref[...].astype(BF16)
        wdn16[...] = wdn_ref[...].astype(BF16)

    def block_step(slot):
        nslot = 1 - slot
        for r in range(blk):
            gather_copy(0, slot, r).wait()
        for r in range(blk):
            scatter_copy(0, slot, r).wait()
        for r in range(blk):
            gather_copy(srcn_ref[0, r], nslot, r).start()
        for r in range(blk):
            scatter_copy(dstp_ref[0, r], nslot, r).start()
        f = wdn16.shape[0]
        x = _from_row_tiles(xbuf[slot]).astype(BF16)
        gu = _dot(x, wgu16[...]) + bgu_ref[...]
        g = jnp.minimum(gu[:, :f], SWIGLU_LIMIT)
        u = jnp.clip(gu[:, f:], -SWIGLU_LIMIT, SWIGLU_LIMIT)
        hid = g * jax.nn.sigmoid(SWIGLU_ALPHA * g) * (u + 1.0)
        _to_row_tiles(ybuf[slot], _dot(hid.astype(BF16), wdn16[...]) + bdn_ref[...])

    for slot in range(2):
        pl.when(jnp.logical_and(i < nb, i % 2 == slot))(functools.partial(block_step, slot))

    @pl.when(i == nb)
    def _():
        slot = nb % 2
        nslot = 1 - slot
        for r in range(blk):
            gather_copy(0, slot, r).wait()
        for r in range(blk):
            scatter_copy(0, slot, r).wait()
        for r in range(blk):
            scatter_copy(dstp_ref[0, r], nslot, r).start()
        for r in range(blk):
            scatter_copy(0, nslot, r).wait()


def _moe(x3, layer, block_e, src_tok, dst_row, w_gu, b_gu, w_dn, b_dn):
    _, n_e, d, f2 = w_gu.shape
    assert x3.shape[1:] == (SUBLANES, LANES) and d == SUBLANES * LANES
    f = f2 // 2
    blk = MOE_BLOCK
    nb = src_tok.shape[0]
    last = nb - 1
    spare1 = (nb + 1) * blk + jnp.arange(blk, dtype=I32).reshape(1, 1, blk)
    dst_late = jnp.concatenate([spare1, dst_row], axis=0)
    grid_spec = pltpu.PrefetchScalarGridSpec(
        num_scalar_prefetch=1,
        grid=(nb + 1,),
        in_specs=[
            pl.BlockSpec((None, 1, blk), lambda i, be: (jnp.minimum(i, last), 0, 0), memory_space=pltpu.SMEM),
            pl.BlockSpec((None, 1, blk), lambda i, be: (jnp.minimum(i + 1, last), 0, 0), memory_space=pltpu.SMEM),
            pl.BlockSpec((None, 1, blk), lambda i, be: (i, 0, 0), memory_space=pltpu.SMEM),
            pl.BlockSpec(memory_space=pl.ANY),
            pl.BlockSpec((None, None, d, f2), lambda i, be: (layer, be[jnp.minimum(i, last)], 0, 0)),
            pl.BlockSpec((None, None, 1, f2), lambda i, be: (layer, be[jnp.minimum(i, last)], 0, 0)),
            pl.BlockSpec((None, None, f, d), lambda i, be: (layer, be[jnp.minimum(i, last)], 0, 0)),
            pl.BlockSpec((None, None, 1, d), lambda i, be: (layer, be[jnp.minimum(i, last)], 0, 0)),
        ],
        out_specs=pl.BlockSpec(memory_space=pl.ANY),
        scratch_shapes=[pltpu.VMEM((blk * SUBLANES, LANES), F32)] * 4 + [
                        pltpu.VMEM((d, f2), BF16), pltpu.VMEM((f, d), BF16),
                        pltpu.SemaphoreType.DMA((2,)), pltpu.SemaphoreType.DMA((2,))],
    )
    return pl.pallas_call(
        functools.partial(_moe_kernel, blk, nb),
        out_shape=jax.ShapeDtypeStruct(((nb + 2) * blk, SUBLANES, LANES), F32),
        grid_spec=grid_spec,
        compiler_params=_cparams(1, 56),
        name="moe_experts",
    )(block_e, src_tok, src_tok, dst_late, x3, w_gu, b_gu.reshape(b_gu.shape[0], n_e, 1, f2),
      w_dn, b_dn.reshape(b_dn.shape[0], n_e, 1, d))


def _combine_ln_kernel(nblk_p, split, *refs):
    y_refs = refs[:TOP_K]
    tg_ref, x_ref, g_ref, b_ref = refs[TOP_K:TOP_K + 4]
    out_refs = refs[TOP_K + 4:]
    tg = tg_ref[...]
    y = tg[:, 0:1] * _from_row_tiles(y_refs[0])
    for k in range(1, TOP_K):
        y = y + tg[:, k:k + 1] * _from_row_tiles(y_refs[k])
    out = _layernorm(DEEPNORM_ALPHA * x_ref[...] + y, g_ref[...], b_ref[...])
    if split:
        is_p = pl.program_id(0) < nblk_p

        @pl.when(is_p)
        def _():
            out_refs[0][...] = out

        @pl.when(jnp.logical_not(is_p))
        def _():
            out_refs[1][...] = out
    else:
        out_refs[0][...] = out


def _combine_ln(y_rows, tg, x, g, b, n_p, split):
    n_t, d = x.shape
    tm = ROW_TILE
    nblk_p = n_p // tm
    nblk = n_t // tm
    y_rows = y_rows.reshape(-1, LANES)
    y_specs = [pl.BlockSpec((tm * SUBLANES, LANES), functools.partial(lambda k, i: (k * nblk + i, 0), k))
               for k in range(TOP_K)]
    if split:
        out_shape = (jax.ShapeDtypeStruct((n_p, d), F32), jax.ShapeDtypeStruct((n_t - n_p, d), F32))
        out_specs = (_rows_a(tm, d, nblk_p), _rows_b(tm, d, nblk_p, 0))
    else:
        out_shape = jax.ShapeDtypeStruct((n_t, d), F32)
        out_specs = _rows(tm, d)
    return pl.pallas_call(
        functools.partial(_combine_ln_kernel, nblk_p, split),
        out_shape=out_shape,
        grid=(n_t // tm,),
        in_specs=y_specs + [_rows(tm, LANES), _rows(tm, d), _full((1, d)), _full((1, d))],
        out_specs=out_specs,
        compiler_params=_cparams(1, 32),
        name="combine_ln",
    )(*([y_rows] * TOP_K), tg, x, g.reshape(1, d), b.reshape(1, d))


def _moe_layer(x, x3, layer, n_p, split, ln_g, ln_b, w_r, b_r, w_gu, b_gu, w_dn, b_dn):
    n_e = w_r.shape[-1]
    ti, tg, cnt = _router(x, w_r[layer], b_r[layer])
    block_e, src_tok, dst_row = _moe_plan(ti[:, :TOP_K], cnt[0, :n_e].astype(I32))
    y_rows = _moe(x3.reshape(-1, SUBLANES, LANES), layer, block_e, src_tok, dst_row, w_gu, b_gu, w_dn, b_dn)
    return _combine_ln(y_rows, tg, x, ln_g, ln_b, n_p, split)


def _mla_proj_kernel(nblk_p, q_lora, kv_lora, n_heads, x_ref, w1_ref, gq_ref, w2a_ref, w2b_ref, gkv_ref,
                     qc_ref, qs_ref, kc_ref, ks_ref,
                     q_ref, ckvp_ref, ckvs_ref, kpep_ref, kpes_ref, ckv16_ref, kpe16_ref):
    is_p = pl.program_id(0) < nblk_p
    x = x_ref[...].astype(BF16)
    y1 = _dot(x, w1_ref[...])
    a = _rmsnorm(y1[:, :q_lora], gq_ref[...]).astype(BF16)
    qa = _dot(a, w2a_ref[...])
    qb = _dot(a, w2b_ref[...])
    qc = qc_ref[...]
    qs = qs_ref[...]
    for h in range(n_heads):
        cols = slice(h * LANES, (h + 1) * LANES)
        q_ref[:, cols] = (qa[:, cols] * qc + qb[:, cols] * qs).astype(BF16)
    ckv = _rmsnorm(y1[:, q_lora:q_lora + kv_lora], gkv_ref[...])
    c0 = q_lora + kv_lora
    kpe = y1[:, c0:c0 + LANES] * kc_ref[...] + y1[:, c0 + LANES:c0 + 2 * LANES] * ks_ref[...]
    ckv16_ref[...] = ckv.astype(BF16)
    kpe16_ref[...] = kpe.astype(BF16)
    rope = kpep_ref.shape[-1]

    @pl.when(is_p)
    def _():
        ckvp_ref[...] = ckv
        kpep_ref[...] = kpe[:, :rope]

    @pl.when(jnp.logical_not(is_p))
    def _():
        ckvs_ref[...] = ckv
        kpes_ref[...] = kpe[:, :rope]


def _rot_cols(w):
    half = w.shape[-1] // 2
    return jnp.concatenate([-w[..., half:], w[..., :half]], axis=-1)


def _mla_proj(x, n_p, t_p, t_s, past, w_dq, g_q, w_uq, w_dkv, g_kv):
    n_t, d = x.shape
    n_s = n_t - n_p
    tm = ROW_TILE
    nblk_p = n_p // tm
    q_lora = w_dq.shape[-1]
    kv_lora = g_kv.shape[-1]
    hd = MLA_NOPE + MLA_ROPE
    n_heads = w_uq.shape[-1] // hd
    assert q_lora % LANES == 0 and kv_lora % LANES == 0 and t_p % tm == 0 and tm % t_s == 0

    w_pe = w_dkv[:, kv_lora:]
    pad = jnp.zeros((d, LANES - MLA_ROPE), F32)
    w1 = jnp.concatenate([w_dq, w_dkv[:, :kv_lora], w_pe, pad, _rot_cols(w_pe), pad], axis=1).astype(BF16)

    w3 = w_uq.reshape(q_lora, n_heads, hd)
    zq = jnp.zeros((q_lora, n_heads, LANES - hd), F32)
    w2a = jnp.concatenate([w3, zq], axis=2).reshape(q_lora, n_heads * LANES).astype(BF16)
    w2b = jnp.concatenate([jnp.zeros((q_lora, n_heads, MLA_NOPE), F32), _rot_cols(w3[:, :, MLA_NOPE:]), zq],
                          axis=2).reshape(q_lora, n_heads * LANES).astype(BF16)

    half = MLA_ROPE // 2
    inv_freq = ROPE_THETA ** (-jnp.arange(half, dtype=F32) / half)
    pos = jnp.concatenate([jnp.arange(t_p), past + (jnp.arange(tm) % t_s)]).astype(F32)
    ang = pos[:, None] * inv_freq[None, :]
    cos2 = jnp.concatenate([jnp.cos(ang), jnp.cos(ang)], axis=1)
    sin2 = jnp.concatenate([jnp.sin(ang), jnp.sin(ang)], axis=1)
    rows = pos.shape[0]
    scale = hd ** -0.5
    z32 = jnp.zeros((rows, LANES - hd), F32)
    q_cos = jnp.concatenate([jnp.full((rows, MLA_NOPE), scale, F32), scale * cos2, z32], axis=1)
    q_sin = jnp.concatenate([jnp.zeros((rows, MLA_NOPE), F32), scale * sin2, z32], axis=1)
    z96 = jnp.zeros((rows, LANES - MLA_ROPE), F32)
    k_cos = jnp.concatenate([cos2, z96], axis=1)
    k_sin = jnp.concatenate([sin2, z96], axis=1)
    tpb = t_p // tm
    tbl = pl.BlockSpec((tm, LANES), lambda i: (jnp.where(i < nblk_p, i % tpb, tpb), 0))

    w1w = w1.shape[1]
    return pl.pallas_call(
        functools.partial(_mla_proj_kernel, nblk_p, q_lora, kv_lora, n_heads),
        out_shape=(jax.ShapeDtypeStruct((n_t, n_heads * LANES), BF16),
                   jax.ShapeDtypeStruct((n_p, kv_lora), F32), jax.ShapeDtypeStruct((n_s, kv_lora), F32),
                   jax.ShapeDtypeStruct((n_p, MLA_ROPE), F32), jax.ShapeDtypeStruct((n_s, MLA_ROPE), F32),
                   jax.ShapeDtypeStruct((n_t, kv_lora), BF16), jax.ShapeDtypeStruct((n_t, LANES), BF16)),
        grid=(n_t // tm,),
        in_specs=[_rows(tm, d), _full((d, w1w)), _full((1, q_lora)),
                  _full((q_lora, n_heads * LANES)), _full((q_lora, n_heads * LANES)), _full((1, kv_lora)),
                  tbl, tbl, tbl, tbl],
        out_specs=(_rows(tm, n_heads * LANES),
                   _rows_a(tm, kv_lora, nblk_p), _rows_b(tm, kv_lora, nblk_p, 0),
                   _rows_a(tm, MLA_ROPE, nblk_p), _rows_b(tm, MLA_ROPE, nblk_p, 0),
                   _rows(tm, kv_lora), _rows(tm, LANES)),
        compiler_params=_cparams(1, 48),
        name="mla_proj",
    )(x, w1, g_q.reshape(1, q_lora), w2a, w2b, g_kv.reshape(1, kv_lora), q_cos, q_sin, k_cos, k_sin)


def _mla_expand_kernel(ckv_ref, kpe_ref, wk_ref, pk_ref, wv_ref, k_ref, v_ref):
    ckv = ckv_ref[...]
    k_ref[...] = (_dot(ckv, wk_ref[...]) + _dot(kpe_ref[...], pk_ref[...])).astype(BF16)
    v_ref[...] = _dot(ckv, wv_ref[...]).astype(BF16)


def _mla_expand(ckv16, kpe16, n_rows, w_ukv):
    kv_lora = ckv16.shape[-1]
    hd = MLA_NOPE + MLA_V
    n_heads = w_ukv.shape[-1] // hd
    tm = next(t for t in (512, 256, LANES) if n_rows % t == 0)
    w3 = w_ukv.reshape(kv_lora, n_heads, hd)
    wk = jnp.concatenate([w3[:, :, :MLA_NOPE], jnp.zeros((kv_lora, n_heads, LANES - MLA_NOPE), F32)],
                         axis=2).reshape(kv_lora, n_heads * LANES).astype(BF16)
    wv = w3[:, :, MLA_NOPE:].reshape(kv_lora, n_heads * MLA_V).astype(BF16)
    pk = np.zeros((LANES, n_heads, LANES), np.float32)
    for j in range(MLA_ROPE):
        pk[j, :, MLA_NOPE + j] = 1.0
    pk = jnp.asarray(pk.reshape(LANES, n_heads * LANES), BF16)
    return pl.pallas_call(
        _mla_expand_kernel,
        out_shape=(jax.ShapeDtypeStruct((n_rows, n_heads * LANES), BF16),
                   jax.ShapeDtypeStruct((n_rows, n_heads * MLA_V), BF16)),
        grid=(n_rows // tm,),
        in_specs=[_rows(tm, kv_lora), _rows(tm, LANES), _full((kv_lora, n_heads * LANES)),
                  _full((LANES, n_heads * LANES)), _full((kv_lora, n_heads * MLA_V))],
        out_specs=(_rows(tm, n_heads * LANES), _rows(tm, n_heads * MLA_V)),
        compiler_params=_cparams(1, 32),
        name="mla_expand",
    )(ckv16, kpe16, wk, pk, wv)


def _pad_keys(cache, new, t_pad):
    b, t_c, w = cache.shape
    t_n = new.shape[1]
    parts = [cache.astype(BF16), new.astype(BF16)]
    if t_pad > t_c + t_n:
        parts.append(jnp.zeros((b, t_pad - t_c - t_n, w), BF16))
    return jnp.concatenate(parts, axis=1).reshape(b * t_pad, w)


def kernel(x_prompt, x_sample, cache_fox_k, cache_fox_v, cache_fox_logf, cache_mla_ckv, cache_mla_kpe, fox_w_qkv, fox_w_f, fox_b_f, fox_w_o, mla_w_dq, mla_g_q, mla_w_uq, mla_w_dkv, mla_g_kv, mla_w_ukv, mla_w_o, ln_g, ln_b, moe_w_router, moe_b_router, moe_w_gu, moe_b_gu, moe_w_down, moe_b_down):
    b_p, t_p, d = x_prompt.shape
    b_s, t_s, _ = x_sample.shape
    past = cache_fox_k.shape[2]
    n_p, n_s = b_p * t_p, b_s * t_s
    assert fox_w_qkv.shape[0] == 1 and mla_w_dq.shape[0] == 1 and ln_g.shape[0] == DEPTH
    tq_p = min(ATTN_TILE, t_p)
    assert n_p % ROW_TILE == 0 and n_s % ROW_TILE == 0 and t_p % tq_p == 0 and tq_p % ATTN_SUB == 0
    xp = x_prompt.reshape(n_p, d)
    xs = x_sample.reshape(n_s, d)
    moe_w = (moe_w_router, moe_b_router, moe_w_gu, moe_b_gu, moe_w_down, moe_b_down)
    t_all = -(-(past + t_s) // LANES) * LANES

    n_fh = fox_w_f.shape[-1]
    fd = d // n_fh
    assert 2 * fd == LANES
    qkv16, k_p, k_s, v_p, v_s, logf = _fox_proj(xp, xs, fox_w_qkv[0], fox_w_f[0], fox_b_f[0])
    logf_p = logf[:n_p].reshape(b_p, t_p, n_fh)
    logf_s = logf[n_p:].reshape(b_s, t_s, n_fh)
    qaug_p, kaug_p = _fox_aug(logf_p)
    logf_all = jnp.concatenate([cache_fox_logf[0].astype(F32), logf_s,
                                jnp.zeros((b_s, t_all - past - t_s, n_fh), F32)], axis=1)
    qaug_s, kaug_s = _fox_aug(logf_all)
    n_pairs = n_fh // 2
    o_p = _attention(True, qkv16, qkv16, qkv16, (qaug_p, kaug_p), n_batch=b_p, n_pairs=n_pairs,
                     t_q=t_p, t_k=t_p, tq=tq_p, tk=tq_p, q_off=0, q_row0=0,
                     q_col0=0, k_col0=n_pairs, v_col0=2 * n_pairs, out_rows=n_p, name="fox_attn_prompt")
    k_all = _pad_keys(cache_fox_k[0].reshape(b_s, past, d), k_s.reshape(b_s, t_s, d), t_all)
    v_all = _pad_keys(cache_fox_v[0].reshape(b_s, past, d), v_s.reshape(b_s, t_s, d), t_all)
    o_s = _attention(True, qkv16, k_all, v_all, (qaug_s, kaug_s), n_batch=b_s, n_pairs=n_pairs,
                     t_q=t_s, t_k=t_all, tq=t_s, tk=t_all, q_off=past, q_row0=n_p,
                     q_col0=0, k_col0=0, v_col0=0, out_rows=n_s, name="fox_attn_sample")
    x1, x1_rows = _proj_ln(o_p, o_s, (xp, xs), fox_w_o[0], ln_g[0, 0], ln_b[0, 0])
    x2 = _moe_layer(x1, x1_rows, 0, n_p, False, ln_g[0, 1], ln_b[0, 1], *moe_w)

    q16, ckv_p, ckv_s, kpe_p, kpe_s, ckv16, kpe16 = _mla_proj(
        x2, n_p, t_p, t_s, past, mla_w_dq[0], mla_g_q[0], mla_w_uq[0], mla_w_dkv[0], mla_g_kv[0])
    kv_lora = ckv_p.shape[-1]
    n_mh = mla_w_o.shape[1] // MLA_V
    m_pairs = n_mh // 2
    k16_p, v16_p = _mla_expand(ckv16, kpe16, n_p, mla_w_ukv[0])
    o_p = _attention(False, q16, k16_p, v16_p, None, n_batch=b_p, n_pairs=m_pairs,
                     t_q=t_p, t_k=t_p, tq=tq_p, tk=tq_p, q_off=0, q_row0=0,
                     q_col0=0, k_col0=0, v_col0=0, out_rows=n_p, name="mla_attn_prompt")
    ckv_all = _pad_keys(cache_mla_ckv[0], ckv16[n_p:].reshape(b_s, t_s, kv_lora), t_all)
    kpe_cache = jnp.pad(cache_mla_kpe[0], ((0, 0), (0, 0), (0, LANES - MLA_ROPE)))
    kpe_all = _pad_keys(kpe_cache, kpe16[n_p:].reshape(b_s, t_s, LANES), t_all)
    k16_s, v16_s = _mla_expand(ckv_all, kpe_all, b_s * t_all, mla_w_ukv[0])
    o_s = _attention(False, q16, k16_s, v16_s, None, n_batch=b_s, n_pairs=m_pairs,
                     t_q=t_s, t_k=t_all, tq=t_s, tk=t_all, q_off=past, q_row0=n_p,
                     q_col0=0, k_col0=0, v_col0=0, out_rows=n_s, name="mla_attn_sample")
    x3, x3_rows = _proj_ln(o_p, o_s, x2, mla_w_o[0], ln_g[1, 0], ln_b[1, 0])
    y_p, y_s = _moe_layer(x3, x3_rows, 1, n_p, True, ln_g[1, 1], ln_b[1, 1], *moe_w)

    return (y_p.reshape(b_p, t_p, d), y_s.reshape(b_s, t_s, d),
            k_p.reshape(1, b_p, t_p, n_fh, fd), v_p.reshape(1, b_p, t_p, n_fh, fd),
            logf_p.reshape(1, b_p, t_p, n_fh),
            ckv_p.reshape(1, b_p, t_p, kv_lora), kpe_p.reshape(1, b_p, t_p, MLA_ROPE),
            k_s.reshape(1, b_s, t_s, n_fh, fd), v_s.reshape(1, b_s, t_s, n_fh, fd),
            logf_s.reshape(1, b_s, t_s, n_fh),
            ckv_s.reshape(1, b_s, t_s, kv_lora), kpe_s.reshape(1, b_s, t_s, MLA_ROPE))
```

```python
import functools

import numpy as np
import jax
import jax.numpy as jnp
from jax import lax
from jax.experimental import pallas as pl
from jax.experimental.pallas import tpu as pltpu

F32 = jnp.float32
BF16 = jnp.bfloat16
I32 = jnp.int32

DEPTH = 2
CHUNK = 64
CHUNK_LOG2 = 6
assert 1 << CHUNK_LOG2 == CHUNK
MLA_NOPE = 64
MLA_ROPE = 32
MLA_V = 64
ROPE_THETA = 10000.0
TOP_K = 4
SWIGLU_LIMIT = 7.0
SWIGLU_ALPHA = 1.702
DEEPNORM_ALPHA = (2.0 * DEPTH) ** 0.25
LN_EPS = 1e-5
RMS_EPS = 1e-6
NEG_INF = -1e30

LANES = 128
SUBLANES = 8
AUG_COLS = 6
MIB = 1 << 20

ROW_TILE = 256
MOE_BLOCK = 256
ATTN_TILE = 2048
ATTN_SUB = 256


def _cparams(n_axes, vmem_mib):
    return pltpu.CompilerParams(dimension_semantics=("arbitrary",) * n_axes,
                                vmem_limit_bytes=vmem_mib * MIB)


def _full(shape):
    nd = len(shape)
    return pl.BlockSpec(shape, lambda *_: (0,) * nd)


def _rows(tm, width):
    return pl.BlockSpec((tm, width), lambda i: (i, 0))


def _rows_a(tm, width, nblk_a):
    return pl.BlockSpec((tm, width), lambda i: (jnp.minimum(i, nblk_a - 1), 0))


def _rows_b(tm, width, nblk_a, base):
    return pl.BlockSpec((tm, width), lambda i: (base + jnp.maximum(i - nblk_a, 0), 0))


def _split2(x):
    hi = x.astype(BF16)
    lo = (x - hi.astype(F32)).astype(BF16)
    return hi, lo


def _split3(x):
    hi = x.astype(BF16)
    r = x - hi.astype(F32)
    mid = r.astype(BF16)
    lo = (r - mid.astype(F32)).astype(BF16)
    return hi, mid, lo


def _dot(a, b):
    return jnp.dot(a, b, preferred_element_type=F32)


def _layernorm(z, g, b):
    mu = jnp.mean(z, axis=-1, keepdims=True)
    zc = z - mu
    var = jnp.mean(zc * zc, axis=-1, keepdims=True)
    return zc * lax.rsqrt(var + LN_EPS) * g + b


def _rmsnorm(x, g):
    return x * lax.rsqrt(jnp.mean(x * x, axis=-1, keepdims=True) + RMS_EPS) * g


def _fox_proj_kernel(nblk_p, xa_ref, xb_ref, wqkv_ref, wfh_ref, wfl_ref, bf_ref,
                     qkv_ref, kp_ref, ks_ref, vp_ref, vs_ref, logf_ref):
    i = pl.program_id(0)
    is_p = i < nblk_p
    x = jnp.where(is_p, xa_ref[...], xb_ref[...])
    xh, xl = _split2(x)
    qkv = _dot(xh, wqkv_ref[...])
    qkv_ref[...] = qkv.astype(BF16)
    d = kp_ref.shape[-1]
    k = qkv[:, d:2 * d]
    v = qkv[:, 2 * d:3 * d]

    @pl.when(is_p)
    def _():
        kp_ref[...] = k
        vp_ref[...] = v

    @pl.when(jnp.logical_not(is_p))
    def _():
        ks_ref[...] = k
        vs_ref[...] = v

    z = _dot(xh, wfh_ref[...]) + _dot(xl, wfh_ref[...]) + _dot(xh, wfl_ref[...]) + bf_ref[...]
    logf_ref[...] = -(jnp.maximum(-z, 0.0) + jnp.log1p(jnp.exp(-jnp.abs(z))))


def _fox_proj(xp, xs, w_qkv, w_f, b_f):
    n_p, d = xp.shape
    n_s = xs.shape[0]
    n_t = n_p + n_s
    h = w_f.shape[-1]
    tm = ROW_TILE
    nblk_p = n_p // tm
    wfh, wfl = _split2(w_f)
    return pl.pallas_call(
        functools.partial(_fox_proj_kernel, nblk_p),
        out_shape=(jax.ShapeDtypeStruct((n_t, 3 * d), BF16),
                   jax.ShapeDtypeStruct((n_p, d), F32), jax.ShapeDtypeStruct((n_s, d), F32),
                   jax.ShapeDtypeStruct((n_p, d), F32), jax.ShapeDtypeStruct((n_s, d), F32),
                   jax.ShapeDtypeStruct((n_t, h), F32)),
        grid=(n_t // tm,),
        in_specs=[_rows_a(tm, d, nblk_p), _rows_b(tm, d, nblk_p, 0),
                  _full((d, 3 * d)), _full((d, h)), _full((d, h)), _full((1, h))],
        out_specs=(_rows(tm, 3 * d),
                   _rows_a(tm, d, nblk_p), _rows_b(tm, d, nblk_p, 0),
                   _rows_a(tm, d, nblk_p), _rows_b(tm, d, nblk_p, 0),
                   _rows(tm, h)),
        compiler_params=_cparams(1, 48),
        name="fox_proj",
    )(xp, xs, w_qkv.astype(BF16), wfh, wfl, b_f.reshape(1, h))


def _fox_aug_kernel(chunk, logf_ref, place_ref, ones_ref, qaug_ref, kaug_ref):
    t, h = logf_ref.shape
    r = lax.broadcasted_iota(I32, (chunk, chunk), 0)
    c = lax.broadcasted_iota(I32, (chunk, chunk), 1)
    tri = jnp.where(c <= r, 1.0, 0.0).astype(BF16)
    carry = jnp.zeros((1, h), F32)
    for j in range(t // chunk):
        rows = pl.ds(j * chunk, chunk)
        hi, mid, lo = _split3(logf_ref[rows, :])
        cs = _dot(tri, hi) + _dot(tri, mid) + _dot(tri, lo) + carry
        carry = cs[chunk - 1:chunk, :]
        chi, cmid, clo = _split3(cs)
        qa = _dot(chi, place_ref[0]) + _dot(cmid, place_ref[1]) + _dot(clo, place_ref[2]) + ones_ref[0]
        ka = ones_ref[1] - (_dot(chi, place_ref[3]) + _dot(cmid, place_ref[4]) + _dot(clo, place_ref[5]))
        qaug_ref[rows, :] = qa.astype(BF16)
        kaug_ref[rows, :] = ka.astype(BF16)


def _fox_aug(logf3):
    b, t, h = logf3.shape
    assert h * AUG_COLS <= LANES
    chunk = 512 if t % 512 == 0 else LANES
    assert t % chunk == 0
    place = np.zeros((AUG_COLS, h, LANES), np.float32)
    ones = np.zeros((2, 1, LANES), np.float32)
    for hh in range(h):
        for j in range(AUG_COLS):
            place[j, hh, AUG_COLS * hh + j] = 1.0
        ones[0, 0, AUG_COLS * hh + 3:AUG_COLS * hh + 6] = 1.0
        ones[1, 0, AUG_COLS * hh:AUG_COLS * hh + 3] = 1.0
    return pl.pallas_call(
        functools.partial(_fox_aug_kernel, chunk),
        out_shape=(jax.ShapeDtypeStruct((b, t, LANES), BF16), jax.ShapeDtypeStruct((b, t, LANES), BF16)),
        grid=(b,),
        in_specs=[pl.BlockSpec((None, t, h), lambda i: (i, 0, 0)),
                  _full((AUG_COLS, h, LANES)), _full((2, 1, LANES))],
        out_specs=(pl.BlockSpec((None, t, LANES), lambda i: (i, 0, 0)),
                   pl.BlockSpec((None, t, LANES), lambda i: (i, 0, 0))),
        compiler_params=_cparams(1, 32),
        name="fox_aug",
    )(logf3, jnp.asarray(place, BF16), jnp.asarray(ones, F32))


def _attn_kernel(fox, tq, tk, q_off, nq, *refs):
    if fox:
        q_ref, k_ref, v_ref, qaug_ref, kaug_ref, o_ref = refs
    else:
        q_ref, k_ref, v_ref, o_ref = refs
    pair = pl.program_id(1)
    iq = pl.program_id(2)
    half = LANES // 2
    sub = min(tq, ATTN_SUB)
    lane = lax.broadcasted_iota(I32, (1, LANES), 1)
    lo_half = lane < half
    qops = []
    for hh in range(2):
        if fox:
            head = 2 * pair + hh
            amask = (lane >= AUG_COLS * head) & (lane < AUG_COLS * head + AUG_COLS)
            scale = half ** -0.5
            qops.append(jnp.concatenate(
                [jnp.where(lo_half == (hh == 0), q_ref[...], 0.0).astype(BF16) * jnp.asarray(scale, BF16),
                 jnp.where(amask, qaug_ref[...], 0.0).astype(BF16)], axis=1))
        else:
            qops.append(q_ref[:, hh * LANES:(hh + 1) * LANES])

    ones_lo = jnp.where(lo_half, 1.0, 0.0).astype(BF16)
    ones_hi = jnp.where(lo_half, 0.0, 1.0).astype(BF16)

    def attend(iq_s):
        q_first = q_off + iq_s * tq
        kf = (q_first // tk) * tk
        for r in range(tq // sub):
            rsl = slice(r * sub, (r + 1) * sub)
            nd = (r + 1) * sub if tq == tk else tk
            nk = kf + nd
            qpos = q_first + r * sub + lax.broadcasted_iota(I32, (sub, nd), 0)
            kpos = kf + lax.broadcasted_iota(I32, (sub, nd), 1)
            if fox:
                allowed = kpos <= qpos
            else:
                allowed = (lax.shift_right_logical(kpos, CHUNK_LOG2)
                           <= lax.shift_right_logical(qpos, CHUNK_LOG2))
            ps = []
            for hh in range(2):
                if fox:
                    kop = jnp.concatenate([k_ref[:nk, :], kaug_ref[:nk, :]], axis=1)
                else:
                    kop = k_ref[:nk, hh * LANES:(hh + 1) * LANES]
                s = lax.dot_general(qops[hh][rsl], kop, (((1,), (1,)), ((), ())), preferred_element_type=F32)
                s_last = jnp.where(allowed, s[:, kf:], NEG_INF)
                s = s_last if kf == 0 else jnp.concatenate([s[:, :kf], s_last], axis=1)
                m = jnp.max(s, axis=1, keepdims=True)
                ps.append(jnp.exp(s - m).astype(BF16))
            v = v_ref[:nk, :]
            vop = jnp.concatenate(
                [jnp.concatenate([jnp.where(lo_half, v, 0.0).astype(BF16), jnp.broadcast_to(ones_lo, v.shape)], axis=1),
                 jnp.concatenate([jnp.where(lo_half, 0.0, v).astype(BF16), jnp.broadcast_to(ones_hi, v.shape)], axis=1)],
                axis=0)
            acc = _dot(jnp.concatenate(ps, axis=1), vop)
            o_ref[rsl, :] = (acc[:, :LANES] / acc[:, LANES:]).astype(o_ref.dtype)

    for iq_s in range(nq):
        pl.when(iq == iq_s)(functools.partial(attend, iq_s))


def _attention(fox, q_arr, k_arr, v_arr, aug, *, n_batch, n_pairs, t_q, t_k, tq, tk, q_off,
               q_row0, q_col0, k_col0, v_col0, out_rows, name):
    assert t_q % tq == 0 and t_k % tk == 0 and q_off % tq == 0
    assert (tq == tk and q_off % tk == 0) or tk == t_k
    nq = t_q // tq
    qw = LANES if fox else 2 * LANES
    qb0 = q_row0 // tq
    assert q_row0 % tq == 0
    in_specs = [
        pl.BlockSpec((tq, qw), lambda b, p, i: (qb0 + b * nq + i, q_col0 + p)),
        pl.BlockSpec((t_k, qw), lambda b, p, i: (b, k_col0 + p)),
        pl.BlockSpec((t_k, LANES), lambda b, p, i: (b, v_col0 + p)),
    ]
    args = [q_arr, k_arr, v_arr]
    if fox:
        qaug, kaug = aug
        ab0 = q_off // tq
        in_specs += [pl.BlockSpec((None, tq, LANES), lambda b, p, i: (b, ab0 + i, 0)),
                     pl.BlockSpec((None, t_k, LANES), lambda b, p, i: (b, 0, 0))]
        args += [qaug, kaug]
    return pl.pallas_call(
        functools.partial(_attn_kernel, fox, tq, tk, q_off, nq),
        out_shape=jax.ShapeDtypeStruct((out_rows, n_pairs * LANES), BF16),
        grid=(n_batch, n_pairs, nq),
        in_specs=in_specs,
        out_specs=pl.BlockSpec((tq, LANES), lambda b, p, i: (b * nq + i, p)),
        compiler_params=_cparams(3, 48),
        name=name,
    )(*args)


def _to_row_tiles(ref, val):
    rows = val.shape[0]
    for c in range(SUBLANES):
        ref[pl.ds(c, rows, stride=SUBLANES), :] = val[:, c * LANES:(c + 1) * LANES]


def _from_row_tiles(ref):
    rows = ref.shape[0] // SUBLANES
    return jnp.concatenate([ref[pl.ds(c, rows, stride=SUBLANES), :] for c in range(SUBLANES)], axis=1)


def _proj_ln_kernel(nblk_p, oa_ref, ob_ref, xa_ref, xb_ref, w_ref, g_ref, b_ref, out_ref, out3_ref):
    is_p = pl.program_id(0) < nblk_p
    o = jnp.where(is_p, oa_ref[...], ob_ref[...])
    x = jnp.where(is_p, xa_ref[...], xb_ref[...])
    z = DEEPNORM_ALPHA * x + _dot(o, w_ref[...])
    out = _layernorm(z, g_ref[...], b_ref[...])
    out_ref[...] = out
    _to_row_tiles(out3_ref, out)


def _proj_ln(o_p, o_s, x_pair, w_o, g, b):
    n_p, d = o_p.shape
    n_s = o_s.shape[0]
    n_t = n_p + n_s
    tm = ROW_TILE
    nblk_p = n_p // tm
    if isinstance(x_pair, tuple):
        xa, xb = x_pair
        xb_spec = _rows_b(tm, d, nblk_p, 0)
    else:
        xa = xb = x_pair
        xb_spec = _rows_b(tm, d, nblk_p, nblk_p)
    return pl.pallas_call(
        functools.partial(_proj_ln_kernel, nblk_p),
        out_shape=(jax.ShapeDtypeStruct((n_t, d), F32),
                   jax.ShapeDtypeStruct((n_t * SUBLANES, LANES), F32)),
        grid=(n_t // tm,),
        in_specs=[_rows_a(tm, d, nblk_p), _rows_b(tm, d, nblk_p, 0),
                  _rows_a(tm, d, nblk_p), xb_spec,
                  _full((d, d)), _full((1, d)), _full((1, d))],
        out_specs=(_rows(tm, d), _rows(tm * SUBLANES, LANES)),
        compiler_params=_cparams(1, 32),
        name="proj_ln",
    )(o_p, o_s, xa, xb, w_o.astype(BF16), g.reshape(1, d), b.reshape(1, d))


def _router_kernel(x_ref, wh_ref, wl_ref, b_ref, ti_ref, tg_ref, cnt_ref):
    xh, xl = _split2(x_ref[...])
    logits = _dot(xh, wh_ref[...]) + _dot(xl, wh_ref[...]) + _dot(xh, wl_ref[...]) + b_ref[...]
    lane = lax.broadcasted_iota(I32, logits.shape, 1).astype(F32)
    vals, idxs = [], []
    cur = logits
    for _ in range(TOP_K):
        m = jnp.max(cur, axis=1, keepdims=True)
        idx = jnp.min(jnp.where(cur == m, lane, float(LANES)), axis=1, keepdims=True)
        vals.append(m)
        idxs.append(idx)
        cur = jnp.where(lane == idx, -jnp.inf, cur)
    exps = [jnp.exp(v - vals[0]) for v in vals]
    denom = exps[0]
    for e in exps[1:]:
        denom = denom + e
    ti = jnp.zeros(logits.shape, F32)
    tg = jnp.zeros(logits.shape, F32)
    for k in range(TOP_K):
        ti = jnp.where(lane == float(k), idxs[k], ti)
        tg = jnp.where(lane == float(k), exps[k] / denom, tg)
    ti_ref[...] = ti.astype(I32)
    tg_ref[...] = tg

    hot = jnp.zeros(logits.shape, F32)
    for k in range(TOP_K):
        hot = hot + jnp.where(lane == idxs[k], 1.0, 0.0)

    @pl.when(pl.program_id(0) == 0)
    def _():
        cnt_ref[...] = jnp.zeros(cnt_ref.shape, F32)

    cnt_ref[...] += jnp.sum(hot, axis=0, keepdims=True)


def _router(x, w_r, b_r):
    n_t, d = x.shape
    e = w_r.shape[-1]
    tm = 2 * ROW_TILE if n_t % (2 * ROW_TILE) == 0 else ROW_TILE
    w_pad = jnp.zeros((d, LANES), F32).at[:, :e].set(w_r)
    b_pad = jnp.full((1, LANES), -jnp.inf, F32).at[0, :e].set(b_r)
    wh, wl = _split2(w_pad)
    return pl.pallas_call(
        _router_kernel,
        out_shape=(jax.ShapeDtypeStruct((n_t, LANES), I32), jax.ShapeDtypeStruct((n_t, LANES), F32),
                   jax.ShapeDtypeStruct((8, LANES), F32)),
        grid=(n_t // tm,),
        in_specs=[_rows(tm, d), _full((d, LANES)), _full((d, LANES)), _full((1, LANES))],
        out_specs=(_rows(tm, LANES), _rows(tm, LANES), _full((8, LANES))),
        compiler_params=_cparams(1, 32),
        name="router",
    )(x, wh, wl, b_pad)


def _moe_plan(top_i, counts):
    n = top_i.shape[0]
    n_experts = counts.shape[0]
    blk = MOE_BLOCK
    p = n * TOP_K
    assert p % blk == 0
    order = jnp.argsort(top_i.reshape(p), stable=True).astype(I32)
    starts = jnp.cumsum(counts) - counts
    nblk_e = (counts + blk - 1) // blk
    bends = jnp.cumsum(nblk_e)
    n_blocks = p // blk + n_experts
    blk_ids = jnp.arange(n_blocks, dtype=I32)
    block_e = jnp.minimum(jnp.sum((bends[None, :] <= blk_ids[:, None]).astype(I32), axis=1), n_experts - 1)
    b_first = (blk_ids - (bends - nblk_e)[block_e]) * blk
    b_cnt = counts[block_e]
    b_start = starts[block_e]
    r_in = jnp.arange(blk, dtype=I32)[None, :]
    off = b_first[:, None] + r_in
    valid = off < b_cnt[:, None]
    pair = order[jnp.clip(b_start[:, None] + off, 0, p - 1)]
    row = blk_ids[:, None] * blk + r_in
    pad_idx = row - (b_start + b_cnt)[:, None]
    src_tok = jnp.where(valid, pair // TOP_K, 0)
    dst_row = jnp.where(valid, (pair % TOP_K) * n + pair // TOP_K, p + pad_idx)
    return block_e, src_tok.reshape(n_blocks, 1, blk), dst_row.reshape(n_blocks, 1, blk)


def _moe_kernel(blk, nb, be_ref, src_ref, srcn_ref, dstp_ref, x_hbm, wgu_ref, bgu_ref, wdn_ref, bdn_ref,
                y_hbm, xb0, xb1, yb0, yb1, wgu16, wdn16, gsem, ssem):
    i = pl.program_id(0)
    spare0 = nb * blk

    xbuf = (xb0, xb1)
    ybuf = (yb0, yb1)

    def gather_copy(tok, s, r):
        return pltpu.make_async_copy(x_hbm.at[tok], xbuf[s].at[pl.ds(r * SUBLANES, SUBLANES)], gsem.at[s])

    def scatter_copy(row, s, r):
        return pltpu.make_async_copy(ybuf[s].at[pl.ds(r * SUBLANES, SUBLANES)], y_hbm.at[row], ssem.at[s])

    @pl.when(i == 0)
    def _():
        yb0[...] = jnp.zeros(yb0.shape, F32)
        yb1[...] = jnp.zeros(yb1.shape, F32)
        for r in range(blk):
            gather_copy(src_ref[0, r], 0, r).start()
        for r in range(blk):
            scatter_copy(spare0 + r, 0, r).start()

    changed = jnp.logical_or(i == 0, be_ref[jnp.minimum(i, nb - 1)] != be_ref[jnp.clip(i - 1, 0, nb - 1)])

    @pl.when(jnp.logical_and(i < nb, changed))
    def _():
        wgu16[...] = wgu_ref[...].astype(BF16)
        wdn16[...] = wdn_ref[...].astype(BF16)

    def block_step(slot):
        nslot = 1 - slot
        for r in range(blk):
            gather_copy(0, slot, r).wait()
        for r in range(blk):
            scatter_copy(0, slot, r).wait()
        for r in range(blk):
            gather_copy(srcn_ref[0, r], nslot, r).start(priority=r % 2)
        for r in range(blk):
            scatter_copy(dstp_ref[0, r], nslot, r).start(priority=r % 2)
        f = wdn16.shape[0]
        x = _from_row_tiles(xbuf[slot]).astype(BF16)
        gu = _dot(x, wgu16[...]) + bgu_ref[...]
        g = jnp.minimum(gu[:, :f], SWIGLU_LIMIT)
        u = jnp.clip(gu[:, f:], -SWIGLU_LIMIT, SWIGLU_LIMIT)
        hid = g * jax.nn.sigmoid(SWIGLU_ALPHA * g) * (u + 1.0)
        _to_row_tiles(ybuf[slot], _dot(hid.astype(BF16), wdn16[...]) + bdn_ref[...])

    for slot in range(2):
        pl.when(jnp.logical_and(i < nb, i % 2 == slot))(functools.partial(block_step, slot))

    @pl.when(i == nb)
    def _():
        slot = nb % 2
        nslot = 1 - slot
        for r in range(blk):
            gather_copy(0, slot, r).wait()
        for r in range(blk):
            scatter_copy(0, slot, r).wait()
        for r in range(blk):
            scatter_copy(dstp_ref[0, r], nslot, r).start()
        for r in range(blk):
            scatter_copy(0, nslot, r).wait()


def _moe(x3, layer, block_e, src_tok, dst_row, w_gu, b_gu, w_dn, b_dn):
    _, n_e, d, f2 = w_gu.shape
    assert x3.shape[1:] == (SUBLANES, LANES) and d == SUBLANES * LANES
    f = f2 // 2
    blk = MOE_BLOCK
    nb = src_tok.shape[0]
    last = nb - 1
    spare1 = (nb + 1) * blk + jnp.arange(blk, dtype=I32).reshape(1, 1, blk)
    dst_late = jnp.concatenate([spare1, dst_row], axis=0)
    grid_spec = pltpu.PrefetchScalarGridSpec(
        num_scalar_prefetch=1,
        grid=(nb + 1,),
        in_specs=[
            pl.BlockSpec((None, 1, blk), lambda i, be: (jnp.minimum(i, last), 0, 0), memory_space=pltpu.SMEM),
            pl.BlockSpec((None, 1, blk), lambda i, be: (jnp.minimum(i + 1, last), 0, 0), memory_space=pltpu.SMEM),
            pl.BlockSpec((None, 1, blk), lambda i, be: (i, 0, 0), memory_space=pltpu.SMEM),
            pl.BlockSpec(memory_space=pl.ANY),
            pl.BlockSpec((None, None, d, f2), lambda i, be: (layer, be[jnp.minimum(i, last)], 0, 0)),
            pl.BlockSpec((None, None, 1, f2), lambda i, be: (layer, be[jnp.minimum(i, last)], 0, 0)),
            pl.BlockSpec((None, None, f, d), lambda i, be: (layer, be[jnp.minimum(i, last)], 0, 0)),
            pl.BlockSpec((None, None, 1, d), lambda i, be: (layer, be[jnp.minimum(i, last)], 0, 0)),
        ],
        out_specs=pl.BlockSpec(memory_space=pl.ANY),
        scratch_shapes=[pltpu.VMEM((blk * SUBLANES, LANES), F32)] * 4 + [
                        pltpu.VMEM((d, f2), BF16), pltpu.VMEM((f, d), BF16),
                        pltpu.SemaphoreType.DMA((2,)), pltpu.SemaphoreType.DMA((2,))],
    )
    return pl.pallas_call(
        functools.partial(_moe_kernel, blk, nb),
        out_shape=jax.ShapeDtypeStruct(((nb + 2) * blk, SUBLANES, LANES), F32),
        grid_spec=grid_spec,
        compiler_params=_cparams(1, 56),
        name="moe_experts",
    )(block_e, src_tok, src_tok, dst_late, x3, w_gu, b_gu.reshape(b_gu.shape[0], n_e, 1, f2),
      w_dn, b_dn.reshape(b_dn.shape[0], n_e, 1, d))


def _combine_ln_kernel(nblk_p, split, *refs):
    y_refs = refs[:TOP_K]
    tg_ref, x_ref, g_ref, b_ref = refs[TOP_K:TOP_K + 4]
    out_refs = refs[TOP_K + 4:]
    tg = tg_ref[...]
    y = tg[:, 0:1] * _from_row_tiles(y_refs[0])
    for k in range(1, TOP_K):
        y = y + tg[:, k:k + 1] * _from_row_tiles(y_refs[k])
    out = _layernorm(DEEPNORM_ALPHA * x_ref[...] + y, g_ref[...], b_ref[...])
    if split:
        is_p = pl.program_id(0) < nblk_p

        @pl.when(is_p)
        def _():
            out_refs[0][...] = out

        @pl.when(jnp.logical_not(is_p))
        def _():
            out_refs[1][...] = out
    else:
        out_refs[0][...] = out


def _combine_ln(y_rows, tg, x, g, b, n_p, split):
    n_t, d = x.shape
    tm = ROW_TILE
    nblk_p = n_p // tm
    nblk = n_t // tm
    y_rows = y_rows.reshape(-1, LANES)
    y_specs = [pl.BlockSpec((tm * SUBLANES, LANES), functools.partial(lambda k, i: (k * nblk + i, 0), k))
               for k in range(TOP_K)]
    if split:
        out_shape = (jax.ShapeDtypeStruct((n_p, d), F32), jax.ShapeDtypeStruct((n_t - n_p, d), F32))
        out_specs = (_rows_a(tm, d, nblk_p), _rows_b(tm, d, nblk_p, 0))
    else:
        out_shape = jax.ShapeDtypeStruct((n_t, d), F32)
        out_specs = _rows(tm, d)
    return pl.pallas_call(
        functools.partial(_combine_ln_kernel, nblk_p, split),
        out_shape=out_shape,
        grid=(n_t // tm,),
        in_specs=y_specs + [_rows(tm, LANES), _rows(tm, d), _full((1, d)), _full((1, d))],
        out_specs=out_specs,
        compiler_params=_cparams(1, 32),
        name="combine_ln",
    )(*([y_rows] * TOP_K), tg, x, g.reshape(1, d), b.reshape(1, d))


def _moe_layer(x, x3, layer, n_p, split, ln_g, ln_b, w_r, b_r, w_gu, b_gu, w_dn, b_dn):
    n_e = w_r.shape[-1]
    ti, tg, cnt = _router(x, w_r[layer], b_r[layer])
    block_e, src_tok, dst_row = _moe_plan(ti[:, :TOP_K], cnt[0, :n_e].astype(I32))
    y_rows = _moe(x3.reshape(-1, SUBLANES, LANES), layer, block_e, src_tok, dst_row, w_gu, b_gu, w_dn, b_dn)
    return _combine_ln(y_rows, tg, x, ln_g, ln_b, n_p, split)


def _mla_proj_kernel(nblk_p, q_lora, kv_lora, n_heads, x_ref, w1_ref, gq_ref, w2a_ref, w2b_ref, gkv_ref,
                     qc_ref, qs_ref, kc_ref, ks_ref,
                     q_ref, ckvp_ref, ckvs_ref, kpep_ref, kpes_ref, ckv16_ref, kpe16_ref):
    is_p = pl.program_id(0) < nblk_p
    x = x_ref[...].astype(BF16)
    y1 = _dot(x, w1_ref[...])
    a = _rmsnorm(y1[:, :q_lora], gq_ref[...]).astype(BF16)
    qa = _dot(a, w2a_ref[...])
    qb = _dot(a, w2b_ref[...])
    qc = qc_ref[...]
    qs = qs_ref[...]
    for h in range(n_heads):
        cols = slice(h * LANES, (h + 1) * LANES)
        q_ref[:, cols] = (qa[:, cols] * qc + qb[:, cols] * qs).astype(BF16)
    ckv = _rmsnorm(y1[:, q_lora:q_lora + kv_lora], gkv_ref[...])
    c0 = q_lora + kv_lora
    kpe = y1[:, c0:c0 + LANES] * kc_ref[...] + y1[:, c0 + LANES:c0 + 2 * LANES] * ks_ref[...]
    ckv16_ref[...] = ckv.astype(BF16)
    kpe16_ref[...] = kpe.astype(BF16)
    rope = kpep_ref.shape[-1]

    @pl.when(is_p)
    def _():
        ckvp_ref[...] = ckv
        kpep_ref[...] = kpe[:, :rope]

    @pl.when(jnp.logical_not(is_p))
    def _():
        ckvs_ref[...] = ckv
        kpes_ref[...] = kpe[:, :rope]


def _rot_cols(w):
    half = w.shape[-1] // 2
    return jnp.concatenate([-w[..., half:], w[..., :half]], axis=-1)


def _mla_proj(x, n_p, t_p, t_s, past, w_dq, g_q, w_uq, w_dkv, g_kv):
    n_t, d = x.shape
    n_s = n_t - n_p
    tm = ROW_TILE
    nblk_p = n_p // tm
    q_lora = w_dq.shape[-1]
    kv_lora = g_kv.shape[-1]
    hd = MLA_NOPE + MLA_ROPE
    n_heads = w_uq.shape[-1] // hd
    assert q_lora % LANES == 0 and kv_lora % LANES == 0 and t_p % tm == 0 and tm % t_s == 0

    w_pe = w_dkv[:, kv_lora:]
    pad = jnp.zeros((d, LANES - MLA_ROPE), F32)
    w1 = jnp.concatenate([w_dq, w_dkv[:, :kv_lora], w_pe, pad, _rot_cols(w_pe), pad], axis=1).astype(BF16)

    w3 = w_uq.reshape(q_lora, n_heads, hd)
    zq = jnp.zeros((q_lora, n_heads, LANES - hd), F32)
    w2a = jnp.concatenate([w3, zq], axis=2).reshape(q_lora, n_heads * LANES).astype(BF16)
    w2b = jnp.concatenate([jnp.zeros((q_lora, n_heads, MLA_NOPE), F32), _rot_cols(w3[:, :, MLA_NOPE:]), zq],
                          axis=2).reshape(q_lora, n_heads * LANES).astype(BF16)

    half = MLA_ROPE // 2
    inv_freq = ROPE_THETA ** (-jnp.arange(half, dtype=F32) / half)
    pos = jnp.concatenate([jnp.arange(t_p), past + (jnp.arange(tm) % t_s)]).astype(F32)
    ang = pos[:, None] * inv_freq[None, :]
    cos2 = jnp.concatenate([jnp.cos(ang), jnp.cos(ang)], axis=1)
    sin2 = jnp.concatenate([jnp.sin(ang), jnp.sin(ang)], axis=1)
    rows = pos.shape[0]
    scale = hd ** -0.5
    z32 = jnp.zeros((rows, LANES - hd), F32)
    q_cos = jnp.concatenate([jnp.full((rows, MLA_NOPE), scale, F32), scale * cos2, z32], axis=1)
    q_sin = jnp.concatenate([jnp.zeros((rows, MLA_NOPE), F32), scale * sin2, z32], axis=1)
    z96 = jnp.zeros((rows, LANES - MLA_ROPE), F32)
    k_cos = jnp.concatenate([cos2, z96], axis=1)
    k_sin = jnp.concatenate([sin2, z96], axis=1)
    tpb = t_p // tm
    tbl = pl.BlockSpec((tm, LANES), lambda i: (jnp.where(i < nblk_p, i % tpb, tpb), 0))

    w1w = w1.shape[1]
    return pl.pallas_call(
        functools.partial(_mla_proj_kernel, nblk_p, q_lora, kv_lora, n_heads),
        out_shape=(jax.ShapeDtypeStruct((n_t, n_heads * LANES), BF16),
                   jax.ShapeDtypeStruct((n_p, kv_lora), F32), jax.ShapeDtypeStruct((n_s, kv_lora), F32),
                   jax.ShapeDtypeStruct((n_p, MLA_ROPE), F32), jax.ShapeDtypeStruct((n_s, MLA_ROPE), F32),
                   jax.ShapeDtypeStruct((n_t, kv_lora), BF16), jax.ShapeDtypeStruct((n_t, LANES), BF16)),
        grid=(n_t // tm,),
        in_specs=[_rows(tm, d), _full((d, w1w)), _full((1, q_lora)),
                  _full((q_lora, n_heads * LANES)), _full((q_lora, n_heads * LANES)), _full((1, kv_lora)),
                  tbl, tbl, tbl, tbl],
        out_specs=(_rows(tm, n_heads * LANES),
                   _rows_a(tm, kv_lora, nblk_p), _rows_b(tm, kv_lora, nblk_p, 0),
                   _rows_a(tm, MLA_ROPE, nblk_p), _rows_b(tm, MLA_ROPE, nblk_p, 0),
                   _rows(tm, kv_lora), _rows(tm, LANES)),
        compiler_params=_cparams(1, 48),
        name="mla_proj",
    )(x, w1, g_q.reshape(1, q_lora), w2a, w2b, g_kv.reshape(1, kv_lora), q_cos, q_sin, k_cos, k_sin)


def _mla_expand_kernel(ckv_ref, kpe_ref, wk_ref, pk_ref, wv_ref, k_ref, v_ref):
    ckv = ckv_ref[...]
    k_ref[...] = (_dot(ckv, wk_ref[...]) + _dot(kpe_ref[...], pk_ref[...])).astype(BF16)
    v_ref[...] = _dot(ckv, wv_ref[...]).astype(BF16)


def _mla_expand(ckv16, kpe16, n_rows, w_ukv):
    kv_lora = ckv16.shape[-1]
    hd = MLA_NOPE + MLA_V
    n_heads = w_ukv.shape[-1] // hd
    tm = next(t for t in (512, 256, LANES) if n_rows % t == 0)
    w3 = w_ukv.reshape(kv_lora, n_heads, hd)
    wk = jnp.concatenate([w3[:, :, :MLA_NOPE], jnp.zeros((kv_lora, n_heads, LANES - MLA_NOPE), F32)],
                         axis=2).reshape(kv_lora, n_heads * LANES).astype(BF16)
    wv = w3[:, :, MLA_NOPE:].reshape(kv_lora, n_heads * MLA_V).astype(BF16)
    pk = np.zeros((LANES, n_heads, LANES), np.float32)
    for j in range(MLA_ROPE):
        pk[j, :, MLA_NOPE + j] = 1.0
    pk = jnp.asarray(pk.reshape(LANES, n_heads * LANES), BF16)
    return pl.pallas_call(
        _mla_expand_kernel,
        out_shape=(jax.ShapeDtypeStruct((n_rows, n_heads * LANES), BF16),
                   jax.ShapeDtypeStruct((n_rows, n_heads * MLA_V), BF16)),
        grid=(n_rows // tm,),
        in_specs=[_rows(tm, kv_lora), _rows(tm, LANES), _full((kv_lora, n_heads * LANES)),
                  _full((LANES, n_heads * LANES)), _full((kv_lora, n_heads * MLA_V))],
        out_specs=(_rows(tm, n_heads * LANES), _rows(tm, n_heads * MLA_V)),
        compiler_params=_cparams(1, 32),
        name="mla_expand",
    )(ckv16, kpe16, wk, pk, wv)


def _pad_keys(cache, new, t_pad):
    b, t_c, w = cache.shape
    t_n = new.shape[1]
    parts = [cache.astype(BF16), new.astype(BF16)]
    if t_pad > t_c + t_n:
        parts.append(jnp.zeros((b, t_pad - t_c - t_n, w), BF16))
    return jnp.concatenate(parts, axis=1).reshape(b * t_pad, w)


def kernel(x_prompt, x_sample, cache_fox_k, cache_fox_v, cache_fox_logf, cache_mla_ckv, cache_mla_kpe, fox_w_qkv, fox_w_f, fox_b_f, fox_w_o, mla_w_dq, mla_g_q, mla_w_uq, mla_w_dkv, mla_g_kv, mla_w_ukv, mla_w_o, ln_g, ln_b, moe_w_router, moe_b_router, moe_w_gu, moe_b_gu, moe_w_down, moe_b_down):
    b_p, t_p, d = x_prompt.shape
    b_s, t_s, _ = x_sample.shape
    past = cache_fox_k.shape[2]
    n_p, n_s = b_p * t_p, b_s * t_s
    assert fox_w_qkv.shape[0] == 1 and mla_w_dq.shape[0] == 1 and ln_g.shape[0] == DEPTH
    tq_p = min(ATTN_TILE, t_p)
    assert n_p % ROW_TILE == 0 and n_s % ROW_TILE == 0 and t_p % tq_p == 0 and tq_p % ATTN_SUB == 0
    xp = x_prompt.reshape(n_p, d)
    xs = x_sample.reshape(n_s, d)
    moe_w = (moe_w_router, moe_b_router, moe_w_gu, moe_b_gu, moe_w_down, moe_b_down)
    t_all = -(-(past + t_s) // LANES) * LANES

    n_fh = fox_w_f.shape[-1]
    fd = d // n_fh
    assert 2 * fd == LANES
    qkv16, k_p, k_s, v_p, v_s, logf = _fox_proj(xp, xs, fox_w_qkv[0], fox_w_f[0], fox_b_f[0])
    logf_p = logf[:n_p].reshape(b_p, t_p, n_fh)
    logf_s = logf[n_p:].reshape(b_s, t_s, n_fh)
    qaug_p, kaug_p = _fox_aug(logf_p)
    logf_all = jnp.concatenate([cache_fox_logf[0].astype(F32), logf_s,
                                jnp.zeros((b_s, t_all - past - t_s, n_fh), F32)], axis=1)
    qaug_s, kaug_s = _fox_aug(logf_all)
    n_pairs = n_fh // 2
    o_p = _attention(True, qkv16, qkv16, qkv16, (qaug_p, kaug_p), n_batch=b_p, n_pairs=n_pairs,
                     t_q=t_p, t_k=t_p, tq=tq_p, tk=tq_p, q_off=0, q_row0=0,
                     q_col0=0, k_col0=n_pairs, v_col0=2 * n_pairs, out_rows=n_p, name="fox_attn_prompt")
    k_all = _pad_keys(cache_fox_k[0].reshape(b_s, past, d), k_s.reshape(b_s, t_s, d), t_all)
    v_all = _pad_keys(cache_fox_v[0].reshape(b_s, past, d), v_s.reshape(b_s, t_s, d), t_all)
    o_s = _attention(True, qkv16, k_all, v_all, (qaug_s, kaug_s), n_batch=b_s, n_pairs=n_pairs,
                     t_q=t_s, t_k=t_all, tq=t_s, tk=t_all, q_off=past, q_row0=n_p,
                     q_col0=0, k_col0=0, v_col0=0, out_rows=n_s, name="fox_attn_sample")
    x1, x1_rows = _proj_ln(o_p, o_s, (xp, xs), fox_w_o[0], ln_g[0, 0], ln_b[0, 0])
    x2 = _moe_layer(x1, x1_rows, 0, n_p, False, ln_g[0, 1], ln_b[0, 1], *moe_w)

    q16, ckv_p, ckv_s, kpe_p, kpe_s, ckv16, kpe16 = _mla_proj(
        x2, n_p, t_p, t_s, past, mla_w_dq[0], mla_g_q[0], mla_w_uq[0], mla_w_dkv[0], mla_g_kv[0])
    kv_lora = ckv_p.shape[-1]
    n_mh = mla_w_o.shape[1] // MLA_V
    m_pairs = n_mh // 2
    k16_p, v16_p = _mla_expand(ckv16, kpe16, n_p, mla_w_ukv[0])
    o_p = _attention(False, q16, k16_p, v16_p, None, n_batch=b_p, n_pairs=m_pairs,
                     t_q=t_p, t_k=t_p, tq=tq_p, tk=tq_p, q_off=0, q_row0=0,
                     q_col0=0, k_col0=0, v_col0=0, out_rows=n_p, name="mla_attn_prompt")
    ckv_all = _pad_keys(cache_mla_ckv[0], ckv16[n_p:].reshape(b_s, t_s, kv_lora), t_all)
    kpe_cache = jnp.pad(cache_mla_kpe[0], ((0, 0), (0, 0), (0, LANES - MLA_ROPE)))
    kpe_all = _pad_keys(kpe_cache, kpe16[n_p:].reshape(b_s, t_s, LANES), t_all)
    k16_s, v16_s = _mla_expand(ckv_all, kpe_all, b_s * t_all, mla_w_ukv[0])
    o_s = _attention(False, q16, k16_s, v16_s, None, n_batch=b_s, n_pairs=m_pairs,
                     t_q=t_s, t_k=t_all, tq=t_s, tk=t_all, q_off=past, q_row0=n_p,
                     q_col0=0, k_col0=0, v_col0=0, out_rows=n_s, name="mla_attn_sample")
    x3, x3_rows = _proj_ln(o_p, o_s, x2, mla_w_o[0], ln_g[1, 0], ln_b[1, 0])
    y_p, y_s = _moe_layer(x3, x3_rows, 1, n_p, True, ln_g[1, 1], ln_b[1, 1], *moe_w)

    return (y_p.reshape(b_p, t_p, d), y_s.reshape(b_s, t_s, d),
            k_p.reshape(1, b_p, t_p, n_fh, fd), v_p.reshape(1, b_p, t_p, n_fh, fd),
            logf_p.reshape(1, b_p, t_p, n_fh),
            ckv_p.reshape(1, b_p, t_p, kv_lora), kpe_p.reshape(1, b_p, t_p, MLA_ROPE),
            k_s.reshape(1, b_s, t_s, n_fh, fd), v_s.reshape(1, b_s, t_s, n_fh, fd),
            logf_s.reshape(1, b_s, t_s, n_fh),
            ckv_s.reshape(1, b_s, t_s, kv_lora), kpe_s.reshape(1, b_s, t_s, MLA_ROPE))
```

```python
import functools

import numpy as np
import jax
import jax.numpy as jnp
from jax import lax
from jax.experimental import pallas as pl
from jax.experimental.pallas import tpu as pltpu

F32 = jnp.float32
BF16 = jnp.bfloat16
I32 = jnp.int32

DEPTH = 2
CHUNK = 64
CHUNK_LOG2 = 6
assert 1 << CHUNK_LOG2 == CHUNK
MLA_NOPE = 64
MLA_ROPE = 32
MLA_V = 64
ROPE_THETA = 10000.0
TOP_K = 4
SWIGLU_LIMIT = 7.0
SWIGLU_ALPHA = 1.702
DEEPNORM_ALPHA = (2.0 * DEPTH) ** 0.25
LN_EPS = 1e-5
RMS_EPS = 1e-6
NEG_INF = -1e30

LANES = 128
SUBLANES = 8
AUG_COLS = 6
MIB = 1 << 20

ROW_TILE = 256
MOE_BLOCK = 256
ATTN_TILE = 2048
ATTN_SUB = 256
MOE_RING = 3


def _cparams(n_axes, vmem_mib):
    return pltpu.CompilerParams(dimension_semantics=("arbitrary",) * n_axes,
                                vmem_limit_bytes=vmem_mib * MIB)


def _full(shape):
    nd = len(shape)
    return pl.BlockSpec(shape, lambda *_: (0,) * nd)


def _rows(tm, width):
    return pl.BlockSpec((tm, width), lambda i: (i, 0))


def _rows_a(tm, width, nblk_a):
    return pl.BlockSpec((tm, width), lambda i: (jnp.minimum(i, nblk_a - 1), 0))


def _rows_b(tm, width, nblk_a, base):
    return pl.BlockSpec((tm, width), lambda i: (base + jnp.maximum(i - nblk_a, 0), 0))


def _split2(x):
    hi = x.astype(BF16)
    lo = (x - hi.astype(F32)).astype(BF16)
    return hi, lo


def _split3(x):
    hi = x.astype(BF16)
    r = x - hi.astype(F32)
    mid = r.astype(BF16)
    lo = (r - mid.astype(F32)).astype(BF16)
    return hi, mid, lo


def _dot(a, b):
    return jnp.dot(a, b, preferred_element_type=F32)


def _layernorm(z, g, b):
    mu = jnp.mean(z, axis=-1, keepdims=True)
    zc = z - mu
    var = jnp.mean(zc * zc, axis=-1, keepdims=True)
    return zc * lax.rsqrt(var + LN_EPS) * g + b


def _rmsnorm(x, g):
    return x * lax.rsqrt(jnp.mean(x * x, axis=-1, keepdims=True) + RMS_EPS) * g


def _fox_proj_kernel(nblk_p, xa_ref, xb_ref, wqkv_ref, wfh_ref, wfl_ref, bf_ref,
                     qkv_ref, kp_ref, ks_ref, vp_ref, vs_ref, logf_ref):
    i = pl.program_id(0)
    is_p = i < nblk_p
    x = jnp.where(is_p, xa_ref[...], xb_ref[...])
    xh, xl = _split2(x)
    qkv = _dot(xh, wqkv_ref[...])
    qkv_ref[...] = qkv.astype(BF16)
    d = kp_ref.shape[-1]
    k = qkv[:, d:2 * d]
    v = qkv[:, 2 * d:3 * d]

    @pl.when(is_p)
    def _():
        kp_ref[...] = k
        vp_ref[...] = v

    @pl.when(jnp.logical_not(is_p))
    def _():
        ks_ref[...] = k
        vs_ref[...] = v

    z = _dot(xh, wfh_ref[...]) + _dot(xl, wfh_ref[...]) + _dot(xh, wfl_ref[...]) + bf_ref[...]
    logf_ref[...] = -(jnp.maximum(-z, 0.0) + jnp.log1p(jnp.exp(-jnp.abs(z))))


def _fox_proj(xp, xs, w_qkv, w_f, b_f):
    n_p, d = xp.shape
    n_s = xs.shape[0]
    n_t = n_p + n_s
    h = w_f.shape[-1]
    tm = ROW_TILE
    nblk_p = n_p // tm
    wfh, wfl = _split2(w_f)
    return pl.pallas_call(
        functools.partial(_fox_proj_kernel, nblk_p),
        out_shape=(jax.ShapeDtypeStruct((n_t, 3 * d), BF16),
                   jax.ShapeDtypeStruct((n_p, d), F32), jax.ShapeDtypeStruct((n_s, d), F32),
                   jax.ShapeDtypeStruct((n_p, d), F32), jax.ShapeDtypeStruct((n_s, d), F32),
                   jax.ShapeDtypeStruct((n_t, h), F32)),
        grid=(n_t // tm,),
        in_specs=[_rows_a(tm, d, nblk_p), _rows_b(tm, d, nblk_p, 0),
                  _full((d, 3 * d)), _full((d, h)), _full((d, h)), _full((1, h))],
        out_specs=(_rows(tm, 3 * d),
                   _rows_a(tm, d, nblk_p), _rows_b(tm, d, nblk_p, 0),
                   _rows_a(tm, d, nblk_p), _rows_b(tm, d, nblk_p, 0),
                   _rows(tm, h)),
        compiler_params=_cparams(1, 48),
        name="fox_proj",
    )(xp, xs, w_qkv.astype(BF16), wfh, wfl, b_f.reshape(1, h))


def _fox_aug_kernel(chunk, logf_ref, place_ref, ones_ref, qaug_ref, kaug_ref):
    t, h = logf_ref.shape
    r = lax.broadcasted_iota(I32, (chunk, chunk), 0)
    c = lax.broadcasted_iota(I32, (chunk, chunk), 1)
    tri = jnp.where(c <= r, 1.0, 0.0).astype(BF16)
    carry = jnp.zeros((1, h), F32)
    for j in range(t // chunk):
        rows = pl.ds(j * chunk, chunk)
        hi, mid, lo = _split3(logf_ref[rows, :])
        cs = _dot(tri, hi) + _dot(tri, mid) + _dot(tri, lo) + carry
        carry = cs[chunk - 1:chunk, :]
        chi, cmid, clo = _split3(cs)
        qa = _dot(chi, place_ref[0]) + _dot(cmid, place_ref[1]) + _dot(clo, place_ref[2]) + ones_ref[0]
        ka = ones_ref[1] - (_dot(chi, place_ref[3]) + _dot(cmid, place_ref[4]) + _dot(clo, place_ref[5]))
        qaug_ref[rows, :] = qa.astype(BF16)
        kaug_ref[rows, :] = ka.astype(BF16)


def _fox_aug(logf3):
    b, t, h = logf3.shape
    assert h * AUG_COLS <= LANES
    chunk = 512 if t % 512 == 0 else LANES
    assert t % chunk == 0
    place = np.zeros((AUG_COLS, h, LANES), np.float32)
    ones = np.zeros((2, 1, LANES), np.float32)
    for hh in range(h):
        for j in range(AUG_COLS):
            place[j, hh, AUG_COLS * hh + j] = 1.0
        ones[0, 0, AUG_COLS * hh + 3:AUG_COLS * hh + 6] = 1.0
        ones[1, 0, AUG_COLS * hh:AUG_COLS * hh + 3] = 1.0
    return pl.pallas_call(
        functools.partial(_fox_aug_kernel, chunk),
        out_shape=(jax.ShapeDtypeStruct((b, t, LANES), BF16), jax.ShapeDtypeStruct((b, t, LANES), BF16)),
        grid=(b,),
        in_specs=[pl.BlockSpec((None, t, h), lambda i: (i, 0, 0)),
                  _full((AUG_COLS, h, LANES)), _full((2, 1, LANES))],
        out_specs=(pl.BlockSpec((None, t, LANES), lambda i: (i, 0, 0)),
                   pl.BlockSpec((None, t, LANES), lambda i: (i, 0, 0))),
        compiler_params=_cparams(1, 32),
        name="fox_aug",
    )(logf3, jnp.asarray(place, BF16), jnp.asarray(ones, F32))


def _attn_kernel(fox, tq, tk, q_off, nq, *refs):
    if fox:
        q_ref, k_ref, v_ref, qaug_ref, kaug_ref, o_ref = refs
    else:
        q_ref, k_ref, v_ref, o_ref = refs
    pair = pl.program_id(1)
    iq = pl.program_id(2)
    half = LANES // 2
    sub = min(tq, ATTN_SUB)
    lane = lax.broadcasted_iota(I32, (1, LANES), 1)
    lo_half = lane < half
    qops = []
    for hh in range(2):
        if fox:
            head = 2 * pair + hh
            amask = (lane >= AUG_COLS * head) & (lane < AUG_COLS * head + AUG_COLS)
            scale = half ** -0.5
            qops.append(jnp.concatenate(
                [jnp.where(lo_half == (hh == 0), q_ref[...], 0.0).astype(BF16) * jnp.asarray(scale, BF16),
                 jnp.where(amask, qaug_ref[...], 0.0).astype(BF16)], axis=1))
        else:
            qops.append(q_ref[:, hh * LANES:(hh + 1) * LANES])

    ones_lo = jnp.where(lo_half, 1.0, 0.0).astype(BF16)
    ones_hi = jnp.where(lo_half, 0.0, 1.0).astype(BF16)

    def attend(iq_s):
        q_first = q_off + iq_s * tq
        kf = (q_first // tk) * tk
        for r in range(tq // sub):
            rsl = slice(r * sub, (r + 1) * sub)
            nd = (r + 1) * sub if tq == tk else tk
            nk = kf + nd
            qpos = q_first + r * sub + lax.broadcasted_iota(I32, (sub, nd), 0)
            kpos = kf + lax.broadcasted_iota(I32, (sub, nd), 1)
            if fox:
                allowed = kpos <= qpos
            else:
                allowed = (lax.shift_right_logical(kpos, CHUNK_LOG2)
                           <= lax.shift_right_logical(qpos, CHUNK_LOG2))
            ps = []
            for hh in range(2):
                if fox:
                    kop = jnp.concatenate([k_ref[:nk, :], kaug_ref[:nk, :]], axis=1)
                else:
                    kop = k_ref[:nk, hh * LANES:(hh + 1) * LANES]
                s = lax.dot_general(qops[hh][rsl], kop, (((1,), (1,)), ((), ())), preferred_element_type=F32)
                s_last = jnp.where(allowed, s[:, kf:], NEG_INF)
                s = s_last if kf == 0 else jnp.concatenate([s[:, :kf], s_last], axis=1)
                m = jnp.max(s, axis=1, keepdims=True)
                ps.append(jnp.exp(s - m).astype(BF16))
            v = v_ref[:nk, :]
            vop = jnp.concatenate(
                [jnp.concatenate([jnp.where(lo_half, v, 0.0).astype(BF16), jnp.broadcast_to(ones_lo, v.shape)], axis=1),
                 jnp.concatenate([jnp.where(lo_half, 0.0, v).astype(BF16), jnp.broadcast_to(ones_hi, v.shape)], axis=1)],
                axis=0)
            acc = _dot(jnp.concatenate(ps, axis=1), vop)
            o_ref[rsl, :] = (acc[:, :LANES] / acc[:, LANES:]).astype(o_ref.dtype)

    for iq_s in range(nq):
        pl.when(iq == iq_s)(functools.partial(attend, iq_s))


def _attention(fox, q_arr, k_arr, v_arr, aug, *, n_batch, n_pairs, t_q, t_k, tq, tk, q_off,
               q_row0, q_col0, k_col0, v_col0, out_rows, name):
    assert t_q % tq == 0 and t_k % tk == 0 and q_off % tq == 0
    assert (tq == tk and q_off % tk == 0) or tk == t_k
    nq = t_q // tq
    qw = LANES if fox else 2 * LANES
    qb0 = q_row0 // tq
    assert q_row0 % tq == 0
    in_specs = [
        pl.BlockSpec((tq, qw), lambda b, p, i: (qb0 + b * nq + i, q_col0 + p)),
        pl.BlockSpec((t_k, qw), lambda b, p, i: (b, k_col0 + p)),
        pl.BlockSpec((t_k, LANES), lambda b, p, i: (b, v_col0 + p)),
    ]
    args = [q_arr, k_arr, v_arr]
    if fox:
        qaug, kaug = aug
        ab0 = q_off // tq
        in_specs += [pl.BlockSpec((None, tq, LANES), lambda b, p, i: (b, ab0 + i, 0)),
                     pl.BlockSpec((None, t_k, LANES), lambda b, p, i: (b, 0, 0))]
        args += [qaug, kaug]
    return pl.pallas_call(
        functools.partial(_attn_kernel, fox, tq, tk, q_off, nq),
        out_shape=jax.ShapeDtypeStruct((out_rows, n_pairs * LANES), BF16),
        grid=(n_batch, n_pairs, nq),
        in_specs=in_specs,
        out_specs=pl.BlockSpec((tq, LANES), lambda b, p, i: (b * nq + i, p)),
        compiler_params=_cparams(3, 48),
        name=name,
    )(*args)


def _to_row_tiles(ref, val):
    rows = val.shape[0]
    for c in range(SUBLANES):
        ref[pl.ds(c, rows, stride=SUBLANES), :] = val[:, c * LANES:(c + 1) * LANES]


def _from_row_tiles(ref):
    rows = ref.shape[0] // SUBLANES
    return jnp.concatenate([ref[pl.ds(c, rows, stride=SUBLANES), :] for c in range(SUBLANES)], axis=1)


def _proj_ln_kernel(nblk_p, oa_ref, ob_ref, xa_ref, xb_ref, w_ref, g_ref, b_ref, out_ref, out3_ref):
    is_p = pl.program_id(0) < nblk_p
    o = jnp.where(is_p, oa_ref[...], ob_ref[...])
    x = jnp.where(is_p, xa_ref[...], xb_ref[...])
    z = DEEPNORM_ALPHA * x + _dot(o, w_ref[...])
    out = _layernorm(z, g_ref[...], b_ref[...])
    out_ref[...] = out
    _to_row_tiles(out3_ref, out)


def _proj_ln(o_p, o_s, x_pair, w_o, g, b):
    n_p, d = o_p.shape
    n_s = o_s.shape[0]
    n_t = n_p + n_s
    tm = ROW_TILE
    nblk_p = n_p // tm
    if isinstance(x_pair, tuple):
        xa, xb = x_pair
        xb_spec = _rows_b(tm, d, nblk_p, 0)
    else:
        xa = xb = x_pair
        xb_spec = _rows_b(tm, d, nblk_p, nblk_p)
    return pl.pallas_call(
        functools.partial(_proj_ln_kernel, nblk_p),
        out_shape=(jax.ShapeDtypeStruct((n_t, d), F32),
                   jax.ShapeDtypeStruct((n_t * SUBLANES, LANES), F32)),
        grid=(n_t // tm,),
        in_specs=[_rows_a(tm, d, nblk_p), _rows_b(tm, d, nblk_p, 0),
                  _rows_a(tm, d, nblk_p), xb_spec,
                  _full((d, d)), _full((1, d)), _full((1, d))],
        out_specs=(_rows(tm, d), _rows(tm * SUBLANES, LANES)),
        compiler_params=_cparams(1, 32),
        name="proj_ln",
    )(o_p, o_s, xa, xb, w_o.astype(BF16), g.reshape(1, d), b.reshape(1, d))


def _router_kernel(x_ref, wh_ref, wl_ref, b_ref, ti_ref, tg_ref, cnt_ref):
    xh, xl = _split2(x_ref[...])
    logits = _dot(xh, wh_ref[...]) + _dot(xl, wh_ref[...]) + _dot(xh, wl_ref[...]) + b_ref[...]
    lane = lax.broadcasted_iota(I32, logits.shape, 1).astype(F32)
    vals, idxs = [], []
    cur = logits
    for _ in range(TOP_K):
        m = jnp.max(cur, axis=1, keepdims=True)
        idx = jnp.min(jnp.where(cur == m, lane, float(LANES)), axis=1, keepdims=True)
        vals.append(m)
        idxs.append(idx)
        cur = jnp.where(lane == idx, -jnp.inf, cur)
    exps = [jnp.exp(v - vals[0]) for v in vals]
    denom = exps[0]
    for e in exps[1:]:
        denom = denom + e
    ti = jnp.zeros(logits.shape, F32)
    tg = jnp.zeros(logits.shape, F32)
    for k in range(TOP_K):
        ti = jnp.where(lane == float(k), idxs[k], ti)
        tg = jnp.where(lane == float(k), exps[k] / denom, tg)
    ti_ref[...] = ti.astype(I32)
    tg_ref[...] = tg

    hot = jnp.zeros(logits.shape, F32)
    for k in range(TOP_K):
        hot = hot + jnp.where(lane == idxs[k], 1.0, 0.0)

    @pl.when(pl.program_id(0) == 0)
    def _():
        cnt_ref[...] = jnp.zeros(cnt_ref.shape, F32)

    cnt_ref[...] += jnp.sum(hot, axis=0, keepdims=True)


def _router(x, w_r, b_r):
    n_t, d = x.shape
    e = w_r.shape[-1]
    tm = 2 * ROW_TILE if n_t % (2 * ROW_TILE) == 0 else ROW_TILE
    w_pad = jnp.zeros((d, LANES), F32).at[:, :e].set(w_r)
    b_pad = jnp.full((1, LANES), -jnp.inf, F32).at[0, :e].set(b_r)
    wh, wl = _split2(w_pad)
    return pl.pallas_call(
        _router_kernel,
        out_shape=(jax.ShapeDtypeStruct((n_t, LANES), I32), jax.ShapeDtypeStruct((n_t, LANES), F32),
                   jax.ShapeDtypeStruct((8, LANES), F32)),
        grid=(n_t // tm,),
        in_specs=[_rows(tm, d), _full((d, LANES)), _full((d, LANES)), _full((1, LANES))],
        out_specs=(_rows(tm, LANES), _rows(tm, LANES), _full((8, LANES))),
        compiler_params=_cparams(1, 32),
        name="router",
    )(x, wh, wl, b_pad)


def _moe_plan(top_i, counts):
    n = top_i.shape[0]
    n_experts = counts.shape[0]
    blk = MOE_BLOCK
    p = n * TOP_K
    assert p % blk == 0
    order = jnp.argsort(top_i.reshape(p), stable=True).astype(I32)
    starts = jnp.cumsum(counts) - counts
    nblk_e = (counts + blk - 1) // blk
    bends = jnp.cumsum(nblk_e)
    n_blocks = p // blk + n_experts
    blk_ids = jnp.arange(n_blocks, dtype=I32)
    block_e = jnp.minimum(jnp.sum((bends[None, :] <= blk_ids[:, None]).astype(I32), axis=1), n_experts - 1)
    b_first = (blk_ids - (bends - nblk_e)[block_e]) * blk
    b_cnt = counts[block_e]
    b_start = starts[block_e]
    r_in = jnp.arange(blk, dtype=I32)[None, :]
    off = b_first[:, None] + r_in
    valid = off < b_cnt[:, None]
    pair = order[jnp.clip(b_start[:, None] + off, 0, p - 1)]
    row = blk_ids[:, None] * blk + r_in
    pad_idx = row - (b_start + b_cnt)[:, None]
    src_tok = jnp.where(valid, pair // TOP_K, 0)
    dst_row = jnp.where(valid, (pair % TOP_K) * n + pair // TOP_K, p + pad_idx)
    return block_e, src_tok.reshape(n_blocks, 1, blk), dst_row.reshape(n_blocks, 1, blk)


def _moe_kernel(blk, nb, be_ref, src_ref, src1_ref, src2_ref, dstp_ref, x_hbm, wgu_ref, bgu_ref, wdn_ref, bdn_ref,
                y_hbm, xb0, xb1, xb2, yb0, yb1, yb2, wgu16, wdn16, gsem, ssem):
    i = pl.program_id(0)
    ring = MOE_RING
    spare0 = nb * blk

    xbuf = (xb0, xb1, xb2)
    ybuf = (yb0, yb1, yb2)

    def gather_copy(tok, s, r):
        return pltpu.make_async_copy(x_hbm.at[tok], xbuf[s].at[pl.ds(r * SUBLANES, SUBLANES)], gsem.at[s])

    def scatter_copy(row, s, r):
        return pltpu.make_async_copy(ybuf[s].at[pl.ds(r * SUBLANES, SUBLANES)], y_hbm.at[row], ssem.at[s])

    @pl.when(i == 0)
    def _():
        for yb in ybuf:
            yb[...] = jnp.zeros(yb.shape, F32)
        for r in range(blk):
            gather_copy(src_ref[0, r], 0, r).start()
        for r in range(blk):
            gather_copy(src1_ref[0, r], 1, r).start()
        for s in range(2):
            for r in range(blk):
                scatter_copy(spare0 + s * blk + r, s, r).start()

    changed = jnp.logical_or(i == 0, be_ref[jnp.minimum(i, nb - 1)] != be_ref[jnp.clip(i - 1, 0, nb - 1)])

    @pl.when(jnp.logical_and(i < nb, changed))
    def _():
        wgu16[...] = wgu_ref[...].astype(BF16)
        wdn16[...] = wdn_ref[...].astype(BF16)

    def block_step(slot):
        far = (slot + 2) % ring
        for r in range(blk):
            gather_copy(0, slot, r).wait()
        for r in range(blk):
            scatter_copy(0, slot, r).wait()
        for r in range(blk):
            gather_copy(src2_ref[0, r], far, r).start(priority=r % 2)
        for r in range(blk):
            scatter_copy(dstp_ref[0, r], far, r).start(priority=r % 2)
        f = wdn16.shape[0]
        x = _from_row_tiles(xbuf[slot]).astype(BF16)
        gu = _dot(x, wgu16[...]) + bgu_ref[...]
        g = jnp.minimum(gu[:, :f], SWIGLU_LIMIT)
        u = jnp.clip(gu[:, f:], -SWIGLU_LIMIT, SWIGLU_LIMIT)
        hid = g * jax.nn.sigmoid(SWIGLU_ALPHA * g) * (u + 1.0)
        _to_row_tiles(ybuf[slot], _dot(hid.astype(BF16), wdn16[...]) + bdn_ref[...])

    for slot in range(ring):
        pl.when(jnp.logical_and(i < nb, i % ring == slot))(functools.partial(block_step, slot))

    @pl.when(i == nb)
    def _():
        for s in (nb % ring, (nb + 1) % ring):
            for r in range(blk):
                gather_copy(0, s, r).wait()
            for r in range(blk):
                scatter_copy(0, s, r).wait()
        last = (nb + 2) % ring
        for r in range(blk):
            scatter_copy(dstp_ref[0, r], last, r).start()
        for r in range(blk):
            scatter_copy(0, last, r).wait()


def _moe(x3, layer, block_e, src_tok, dst_row, w_gu, b_gu, w_dn, b_dn):
    _, n_e, d, f2 = w_gu.shape
    assert x3.shape[1:] == (SUBLANES, LANES) and d == SUBLANES * LANES
    f = f2 // 2
    blk = MOE_BLOCK
    nb = src_tok.shape[0]
    assert nb >= MOE_RING
    last = nb - 1
    spare2 = (nb + 2) * blk + jnp.arange(blk, dtype=I32).reshape(1, 1, blk)
    dst_late = jnp.concatenate([spare2, dst_row], axis=0)
    grid_spec = pltpu.PrefetchScalarGridSpec(
        num_scalar_prefetch=1,
        grid=(nb + 1,),
        in_specs=[
            pl.BlockSpec((None, 1, blk), lambda i, be: (jnp.minimum(i, last), 0, 0), memory_space=pltpu.SMEM),
            pl.BlockSpec((None, 1, blk), lambda i, be: (jnp.minimum(i + 1, last), 0, 0), memory_space=pltpu.SMEM),
            pl.BlockSpec((None, 1, blk), lambda i, be: (jnp.minimum(i + 2, last), 0, 0), memory_space=pltpu.SMEM),
            pl.BlockSpec((None, 1, blk), lambda i, be: (i, 0, 0), memory_space=pltpu.SMEM),
            pl.BlockSpec(memory_space=pl.ANY),
            pl.BlockSpec((None, None, d, f2), lambda i, be: (layer, be[jnp.minimum(i, last)], 0, 0)),
            pl.BlockSpec((None, None, 1, f2), lambda i, be: (layer, be[jnp.minimum(i, last)], 0, 0)),
            pl.BlockSpec((None, None, f, d), lambda i, be: (layer, be[jnp.minimum(i, last)], 0, 0)),
            pl.BlockSpec((None, None, 1, d), lambda i, be: (layer, be[jnp.minimum(i, last)], 0, 0)),
        ],
        out_specs=pl.BlockSpec(memory_space=pl.ANY),
        scratch_shapes=[pltpu.VMEM((blk * SUBLANES, LANES), F32)] * (2 * MOE_RING) + [
                        pltpu.VMEM((d, f2), BF16), pltpu.VMEM((f, d), BF16),
                        pltpu.SemaphoreType.DMA((MOE_RING,)), pltpu.SemaphoreType.DMA((MOE_RING,))],
    )
    return pl.pallas_call(
        functools.partial(_moe_kernel, blk, nb),
        out_shape=jax.ShapeDtypeStruct(((nb + MOE_RING) * blk, SUBLANES, LANES), F32),
        grid_spec=grid_spec,
        compiler_params=_cparams(1, 56),
        name="moe_experts",
    )(block_e, src_tok, src_tok, src_tok, dst_late, x3, w_gu, b_gu.reshape(b_gu.shape[0], n_e, 1, f2),
      w_dn, b_dn.reshape(b_dn.shape[0], n_e, 1, d))


def _combine_ln_kernel(nblk_p, split, *refs):
    y_refs = refs[:TOP_K]
    tg_ref, x_ref, g_ref, b_ref = refs[TOP_K:TOP_K + 4]
    out_refs = refs[TOP_K + 4:]
    tg = tg_ref[...]
    y = tg[:, 0:1] * _from_row_tiles(y_refs[0])
    for k in range(1, TOP_K):
        y = y + tg[:, k:k + 1] * _from_row_tiles(y_refs[k])
    out = _layernorm(DEEPNORM_ALPHA * x_ref[...] + y, g_ref[...], b_ref[...])
    if split:
        is_p = pl.program_id(0) < nblk_p

        @pl.when(is_p)
        def _():
            out_refs[0][...] = out

        @pl.when(jnp.logical_not(is_p))
        def _():
            out_refs[1][...] = out
    else:
        out_refs[0][...] = out


def _combine_ln(y_rows, tg, x, g, b, n_p, split):
    n_t, d = x.shape
    tm = ROW_TILE
    nblk_p = n_p // tm
    nblk = n_t // tm
    y_rows = y_rows.reshape(-1, LANES)
    y_specs = [pl.BlockSpec((tm * SUBLANES, LANES), functools.partial(lambda k, i: (k * nblk + i, 0), k))
               for k in range(TOP_K)]
    if split:
        out_shape = (jax.ShapeDtypeStruct((n_p, d), F32), jax.ShapeDtypeStruct((n_t - n_p, d), F32))
        out_specs = (_rows_a(tm, d, nblk_p), _rows_b(tm, d, nblk_p, 0))
    else:
        out_shape = jax.ShapeDtypeStruct((n_t, d), F32)
        out_specs = _rows(tm, d)
    return pl.pallas_call(
        functools.partial(_combine_ln_kernel, nblk_p, split),
        out_shape=out_shape,
        grid=(n_t // tm,),
        in_specs=y_specs + [_rows(tm, LANES), _rows(tm, d), _full((1, d)), _full((1, d))],
        out_specs=out_specs,
        compiler_params=_cparams(1, 32),
        name="combine_ln",
    )(*([y_rows] * TOP_K), tg, x, g.reshape(1, d), b.reshape(1, d))


def _moe_layer(x, x3, layer, n_p, split, ln_g, ln_b, w_r, b_r, w_gu, b_gu, w_dn, b_dn):
    n_e = w_r.shape[-1]
    ti, tg, cnt = _router(x, w_r[layer], b_r[layer])
    block_e, src_tok, dst_row = _moe_plan(ti[:, :TOP_K], cnt[0, :n_e].astype(I32))
    y_rows = _moe(x3.reshape(-1, SUBLANES, LANES), layer, block_e, src_tok, dst_row, w_gu, b_gu, w_dn, b_dn)
    return _combine_ln(y_rows, tg, x, ln_g, ln_b, n_p, split)


def _mla_proj_kernel(nblk_p, q_lora, kv_lora, n_heads, x_ref, w1_ref, gq_ref, w2a_ref, w2b_ref, gkv_ref,
                     qc_ref, qs_ref, kc_ref, ks_ref,
                     q_ref, ckvp_ref, ckvs_ref, kpep_ref, kpes_ref, ckv16_ref, kpe16_ref):
    is_p = pl.program_id(0) < nblk_p
    x = x_ref[...].astype(BF16)
    y1 = _dot(x, w1_ref[...])
    a = _rmsnorm(y1[:, :q_lora], gq_ref[...]).astype(BF16)
    qa = _dot(a, w2a_ref[...])
    qb = _dot(a, w2b_ref[...])
    qc = qc_ref[...]
    qs = qs_ref[...]
    for h in range(n_heads):
        cols = slice(h * LANES, (h + 1) * LANES)
        q_ref[:, cols] = (qa[:, cols] * qc + qb[:, cols] * qs).astype(BF16)
    ckv = _rmsnorm(y1[:, q_lora:q_lora + kv_lora], gkv_ref[...])
    c0 = q_lora + kv_lora
    kpe = y1[:, c0:c0 + LANES] * kc_ref[...] + y1[:, c0 + LANES:c0 + 2 * LANES] * ks_ref[...]
    ckv16_ref[...] = ckv.astype(BF16)
    kpe16_ref[...] = kpe.astype(BF16)
    rope = kpep_ref.shape[-1]

    @pl.when(is_p)
    def _():
        ckvp_ref[...] = ckv
        kpep_ref[...] = kpe[:, :rope]

    @pl.when(jnp.logical_not(is_p))
    def _():
        ckvs_ref[...] = ckv
        kpes_ref[...] = kpe[:, :rope]


def _rot_cols(w):
    half = w.shape[-1] // 2
    return jnp.concatenate([-w[..., half:], w[..., :half]], axis=-1)


def _mla_proj(x, n_p, t_p, t_s, past, w_dq, g_q, w_uq, w_dkv, g_kv):
    n_t, d = x.shape
    n_s = n_t - n_p
    tm = ROW_TILE
    nblk_p = n_p // tm
    q_lora = w_dq.shape[-1]
    kv_lora = g_kv.shape[-1]
    hd = MLA_NOPE + MLA_ROPE
    n_heads = w_uq.shape[-1] // hd
    assert q_lora % LANES == 0 and kv_lora % LANES == 0 and t_p % tm == 0 and tm % t_s == 0

    w_pe = w_dkv[:, kv_lora:]
    pad = jnp.zeros((d, LANES - MLA_ROPE), F32)
    w1 = jnp.concatenate([w_dq, w_dkv[:, :kv_lora], w_pe, pad, _rot_cols(w_pe), pad], axis=1).astype(BF16)

    w3 = w_uq.reshape(q_lora, n_heads, hd)
    zq = jnp.zeros((q_lora, n_heads, LANES - hd), F32)
    w2a = jnp.concatenate([w3, zq], axis=2).reshape(q_lora, n_heads * LANES).astype(BF16)
    w2b = jnp.concatenate([jnp.zeros((q_lora, n_heads, MLA_NOPE), F32), _rot_cols(w3[:, :, MLA_NOPE:]), zq],
                          axis=2).reshape(q_lora, n_heads * LANES).astype(BF16)

    half = MLA_ROPE // 2
    inv_freq = ROPE_THETA ** (-jnp.arange(half, dtype=F32) / half)
    pos = jnp.concatenate([jnp.arange(t_p), past + (jnp.arange(tm) % t_s)]).astype(F32)
    ang = pos[:, None] * inv_freq[None, :]
    cos2 = jnp.concatenate([jnp.cos(ang), jnp.cos(ang)], axis=1)
    sin2 = jnp.concatenate([jnp.sin(ang), jnp.sin(ang)], axis=1)
    rows = pos.shape[0]
    scale = hd ** -0.5
    z32 = jnp.zeros((rows, LANES - hd), F32)
    q_cos = jnp.concatenate([jnp.full((rows, MLA_NOPE), scale, F32), scale * cos2, z32], axis=1)
    q_sin = jnp.concatenate([jnp.zeros((rows, MLA_NOPE), F32), scale * sin2, z32], axis=1)
    z96 = jnp.zeros((rows, LANES - MLA_ROPE), F32)
    k_cos = jnp.concatenate([cos2, z96], axis=1)
    k_sin = jnp.concatenate([sin2, z96], axis=1)
    tpb = t_p // tm
    tbl = pl.BlockSpec((tm, LANES), lambda i: (jnp.where(i < nblk_p, i % tpb, tpb), 0))

    w1w = w1.shape[1]
    return pl.pallas_call(
        functools.partial(_mla_proj_kernel, nblk_p, q_lora, kv_lora, n_heads),
        out_shape=(jax.ShapeDtypeStruct((n_t, n_heads * LANES), BF16),
                   jax.ShapeDtypeStruct((n_p, kv_lora), F32), jax.ShapeDtypeStruct((n_s, kv_lora), F32),
                   jax.ShapeDtypeStruct((n_p, MLA_ROPE), F32), jax.ShapeDtypeStruct((n_s, MLA_ROPE), F32),
                   jax.ShapeDtypeStruct((n_t, kv_lora), BF16), jax.ShapeDtypeStruct((n_t, LANES), BF16)),
        grid=(n_t // tm,),
        in_specs=[_rows(tm, d), _full((d, w1w)), _full((1, q_lora)),
                  _full((q_lora, n_heads * LANES)), _full((q_lora, n_heads * LANES)), _full((1, kv_lora)),
                  tbl, tbl, tbl, tbl],
        out_specs=(_rows(tm, n_heads * LANES),
                   _rows_a(tm, kv_lora, nblk_p), _rows_b(tm, kv_lora, nblk_p, 0),
                   _rows_a(tm, MLA_ROPE, nblk_p), _rows_b(tm, MLA_ROPE, nblk_p, 0),
                   _rows(tm, kv_lora), _rows(tm, LANES)),
        compiler_params=_cparams(1, 48),
        name="mla_proj",
    )(x, w1, g_q.reshape(1, q_lora), w2a, w2b, g_kv.reshape(1, kv_lora), q_cos, q_sin, k_cos, k_sin)


def _mla_expand_kernel(ckv_ref, kpe_ref, wk_ref, pk_ref, wv_ref, k_ref, v_ref):
    ckv = ckv_ref[...]
    k_ref[...] = (_dot(ckv, wk_ref[...]) + _dot(kpe_ref[...], pk_ref[...])).astype(BF16)
    v_ref[...] = _dot(ckv, wv_ref[...]).astype(BF16)


def _mla_expand(ckv16, kpe16, n_rows, w_ukv):
    kv_lora = ckv16.shape[-1]
    hd = MLA_NOPE + MLA_V
    n_heads = w_ukv.shape[-1] // hd
    tm = next(t for t in (512, 256, LANES) if n_rows % t == 0)
    w3 = w_ukv.reshape(kv_lora, n_heads, hd)
    wk = jnp.concatenate([w3[:, :, :MLA_NOPE], jnp.zeros((kv_lora, n_heads, LANES - MLA_NOPE), F32)],
                         axis=2).reshape(kv_lora, n_heads * LANES).astype(BF16)
    wv = w3[:, :, MLA_NOPE:].reshape(kv_lora, n_heads * MLA_V).astype(BF16)
    pk = np.zeros((LANES, n_heads, LANES), np.float32)
    for j in range(MLA_ROPE):
        pk[j, :, MLA_NOPE + j] = 1.0
    pk = jnp.asarray(pk.reshape(LANES, n_heads * LANES), BF16)
    return pl.pallas_call(
        _mla_expand_kernel,
        out_shape=(jax.ShapeDtypeStruct((n_rows, n_heads * LANES), BF16),
                   jax.ShapeDtypeStruct((n_rows, n_heads * MLA_V), BF16)),
        grid=(n_rows // tm,),
        in_specs=[_rows(tm, kv_lora), _rows(tm, LANES), _full((kv_lora, n_heads * LANES)),
                  _full((LANES, n_heads * LANES)), _full((kv_lora, n_heads * MLA_V))],
        out_specs=(_rows(tm, n_heads * LANES), _rows(tm, n_heads * MLA_V)),
        compiler_params=_cparams(1, 32),
        name="mla_expand",
    )(ckv16, kpe16, wk, pk, wv)


def _pad_keys(cache, new, t_pad):
    b, t_c, w = cache.shape
    t_n = new.shape[1]
    parts = [cache.astype(BF16), new.astype(BF16)]
    if t_pad > t_c + t_n:
        parts.append(jnp.zeros((b, t_pad - t_c - t_n, w), BF16))
    return jnp.concatenate(parts, axis=1).reshape(b * t_pad, w)


def kernel(x_prompt, x_sample, cache_fox_k, cache_fox_v, cache_fox_logf, cache_mla_ckv, cache_mla_kpe, fox_w_qkv, fox_w_f, fox_b_f, fox_w_o, mla_w_dq, mla_g_q, mla_w_uq, mla_w_dkv, mla_g_kv, mla_w_ukv, mla_w_o, ln_g, ln_b, moe_w_router, moe_b_router, moe_w_gu, moe_b_gu, moe_w_down, moe_b_down):
    b_p, t_p, d = x_prompt.shape
    b_s, t_s, _ = x_sample.shape
    past = cache_fox_k.shape[2]
    n_p, n_s = b_p * t_p, b_s * t_s
    assert fox_w_qkv.shape[0] == 1 and mla_w_dq.shape[0] == 1 and ln_g.shape[0] == DEPTH
    tq_p = min(ATTN_TILE, t_p)
    assert n_p % ROW_TILE == 0 and n_s % ROW_TILE == 0 and t_p % tq_p == 0 and tq_p % ATTN_SUB == 0
    xp = x_prompt.reshape(n_p, d)
    xs = x_sample.reshape(n_s, d)
    moe_w = (moe_w_router, moe_b_router, moe_w_gu, moe_b_gu, moe_w_down, moe_b_down)
    t_all = -(-(past + t_s) // LANES) * LANES

    n_fh = fox_w_f.shape[-1]
    fd = d // n_fh
    assert 2 * fd == LANES
    qkv16, k_p, k_s, v_p, v_s, logf = _fox_proj(xp, xs, fox_w_qkv[0], fox_w_f[0], fox_b_f[0])
    logf_p = logf[:n_p].reshape(b_p, t_p, n_fh)
    logf_s = logf[n_p:].reshape(b_s, t_s, n_fh)
    qaug_p, kaug_p = _fox_aug(logf_p)
    logf_all = jnp.concatenate([cache_fox_logf[0].astype(F32), logf_s,
                                jnp.zeros((b_s, t_all - past - t_s, n_fh), F32)], axis=1)
    qaug_s, kaug_s = _fox_aug(logf_all)
    n_pairs = n_fh // 2
    o_p = _attention(True, qkv16, qkv16, qkv16, (qaug_p, kaug_p), n_batch=b_p, n_pairs=n_pairs,
                     t_q=t_p, t_k=t_p, tq=tq_p, tk=tq_p, q_off=0, q_row0=0,
                     q_col0=0, k_col0=n_pairs, v_col0=2 * n_pairs, out_rows=n_p, name="fox_attn_prompt")
    k_all = _pad_keys(cache_fox_k[0].reshape(b_s, past, d), k_s.reshape(b_s, t_s, d), t_all)
    v_all = _pad_keys(cache_fox_v[0].reshape(b_s, past, d), v_s.reshape(b_s, t_s, d), t_all)
    o_s = _attention(True, qkv16, k_all, v_all, (qaug_s, kaug_s), n_batch=b_s, n_pairs=n_pairs,
                     t_q=t_s, t_k=t_all, tq=t_s, tk=t_all, q_off=past, q_row0=n_p,
                     q_col0=0, k_col0=0, v_col0=0, out_rows=n_s, name="fox_attn_sample")
    x1, x1_rows = _proj_ln(o_p, o_s, (xp, xs), fox_w_o[0], ln_g[0, 0], ln_b[0, 0])
    x2 = _moe_layer(x1, x1_rows, 0, n_p, False, ln_g[0, 1], ln_b[0, 1], *moe_w)

    q16, ckv_p, ckv_s, kpe_p, kpe_s, ckv16, kpe16 = _mla_proj(
        x2, n_p, t_p, t_s, past, mla_w_dq[0], mla_g_q[0], mla_w_uq[0], mla_w_dkv[0], mla_g_kv[0])
    kv_lora = ckv_p.shape[-1]
    n_mh = mla_w_o.shape[1] // MLA_V
    m_pairs = n_mh // 2
    k16_p, v16_p = _mla_expand(ckv16, kpe16, n_p, mla_w_ukv[0])
    o_p = _attention(False, q16, k16_p, v16_p, None, n_batch=b_p, n_pairs=m_pairs,
                     t_q=t_p, t_k=t_p, tq=tq_p, tk=tq_p, q_off=0, q_row0=0,
                     q_col0=0, k_col0=0, v_col0=0, out_rows=n_p, name="mla_attn_prompt")
    ckv_all = _pad_keys(cache_mla_ckv[0], ckv16[n_p:].reshape(b_s, t_s, kv_lora), t_all)
    kpe_cache = jnp.pad(cache_mla_kpe[0], ((0, 0), (0, 0), (0, LANES - MLA_ROPE)))
    kpe_all = _pad_keys(kpe_cache, kpe16[n_p:].reshape(b_s, t_s, LANES), t_all)
    k16_s, v16_s = _mla_expand(ckv_all, kpe_all, b_s * t_all, mla_w_ukv[0])
    o_s = _attention(False, q16, k16_s, v16_s, None, n_batch=b_s, n_pairs=m_pairs,
                     t_q=t_s, t_k=t_all, tq=t_s, tk=t_all, q_off=past, q_row0=n_p,
                     q_col0=0, k_col0=0, v_col0=0, out_rows=n_s, name="mla_attn_sample")
    x3, x3_rows = _proj_ln(o_p, o_s, x2, mla_w_o[0], ln_g[1, 0], ln_b[1, 0])
    y_p, y_s = _moe_layer(x3, x3_rows, 1, n_p, True, ln_g[1, 1], ln_b[1, 1], *moe_w)

    return (y_p.reshape(b_p, t_p, d), y_s.reshape(b_s, t_s, d),
            k_p.reshape(1, b_p, t_p, n_fh, fd), v_p.reshape(1, b_p, t_p, n_fh, fd),
            logf_p.reshape(1, b_p, t_p, n_fh),
            ckv_p.reshape(1, b_p, t_p, kv_lora), kpe_p.reshape(1, b_p, t_p, MLA_ROPE),
            k_s.reshape(1, b_s, t_s, n_fh, fd), v_s.reshape(1, b_s, t_s, n_fh, fd),
            logf_s.reshape(1, b_s, t_s, n_fh),
            ckv_s.reshape(1, b_s, t_s, kv_lora), kpe_s.reshape(1, b_s, t_s, MLA_ROPE))
```

```python
import functools

import numpy as np
import jax
import jax.numpy as jnp
from jax import lax
from jax.experimental import pallas as pl
from jax.experimental.pallas import tpu as pltpu

F32 = jnp.float32
BF16 = jnp.bfloat16
I32 = jnp.int32

DEPTH = 2
CHUNK = 64
CHUNK_LOG2 = 6
assert 1 << CHUNK_LOG2 == CHUNK
MLA_NOPE = 64
MLA_ROPE = 32
MLA_V = 64
ROPE_THETA = 10000.0
TOP_K = 4
SWIGLU_LIMIT = 7.0
SWIGLU_ALPHA = 1.702
DEEPNORM_ALPHA = (2.0 * DEPTH) ** 0.25
LN_EPS = 1e-5
RMS_EPS = 1e-6
NEG_INF = -1e30

LANES = 128
SUBLANES = 8
AUG_COLS = 6
MIB = 1 << 20

ROW_TILE = 256
MOE_BLOCK = 256
ATTN_TILE = 2048
ATTN_SUB = 256
MOE_RING = 3


def _cparams(n_axes, vmem_mib):
    return pltpu.CompilerParams(dimension_semantics=("arbitrary",) * n_axes,
                                vmem_limit_bytes=vmem_mib * MIB)


def _full(shape):
    nd = len(shape)
    return pl.BlockSpec(shape, lambda *_: (0,) * nd)


def _rows(tm, width):
    return pl.BlockSpec((tm, width), lambda i: (i, 0))


def _rows_a(tm, width, nblk_a):
    return pl.BlockSpec((tm, width), lambda i: (jnp.minimum(i, nblk_a - 1), 0))


def _rows_b(tm, width, nblk_a, base):
    return pl.BlockSpec((tm, width), lambda i: (base + jnp.maximum(i - nblk_a, 0), 0))


def _split2(x):
    hi = x.astype(BF16)
    lo = (x - hi.astype(F32)).astype(BF16)
    return hi, lo


def _split3(x):
    hi = x.astype(BF16)
    r = x - hi.astype(F32)
    mid = r.astype(BF16)
    lo = (r - mid.astype(F32)).astype(BF16)
    return hi, mid, lo


def _dot(a, b):
    return jnp.dot(a, b, preferred_element_type=F32)


def _layernorm(z, g, b):
    mu = jnp.mean(z, axis=-1, keepdims=True)
    zc = z - mu
    var = jnp.mean(zc * zc, axis=-1, keepdims=True)
    return zc * lax.rsqrt(var + LN_EPS) * g + b


def _rmsnorm(x, g):
    return x * lax.rsqrt(jnp.mean(x * x, axis=-1, keepdims=True) + RMS_EPS) * g


def _fox_proj_kernel(nblk_p, xa_ref, xb_ref, wqkv_ref, wfh_ref, bf_ref,
                     qkv_ref, kp_ref, ks_ref, vp_ref, vs_ref, logf_ref):
    i = pl.program_id(0)
    is_p = i < nblk_p
    x = jnp.where(is_p, xa_ref[...], xb_ref[...])
    xh, xl = _split2(x)
    y = _dot(xh, wqkv_ref[...])
    d = kp_ref.shape[-1]
    h = logf_ref.shape[-1]
    qkv = y[:, :3 * d]
    qkv_ref[...] = qkv.astype(BF16)
    k = qkv[:, d:2 * d]
    v = qkv[:, 2 * d:3 * d]

    @pl.when(is_p)
    def _():
        kp_ref[...] = k
        vp_ref[...] = v

    @pl.when(jnp.logical_not(is_p))
    def _():
        ks_ref[...] = k
        vs_ref[...] = v

    z = (y[:, 3 * d:3 * d + h] + y[:, 3 * d + LANES:3 * d + LANES + h]
         + _dot(xl, wfh_ref[...]) + bf_ref[...])
    logf_ref[...] = -(jnp.maximum(-z, 0.0) + jnp.log1p(jnp.exp(-jnp.abs(z))))


def _fox_proj(xp, xs, w_qkv, w_f, b_f):
    n_p, d = xp.shape
    n_s = xs.shape[0]
    n_t = n_p + n_s
    h = w_f.shape[-1]
    tm = ROW_TILE
    nblk_p = n_p // tm
    wfh, wfl = _split2(w_f)
    pad = jnp.zeros((d, LANES - h), BF16)
    w_cat = jnp.concatenate([w_qkv.astype(BF16), wfh, pad, wfl, pad], axis=1)
    wc = w_cat.shape[1]
    return pl.pallas_call(
        functools.partial(_fox_proj_kernel, nblk_p),
        out_shape=(jax.ShapeDtypeStruct((n_t, 3 * d), BF16),
                   jax.ShapeDtypeStruct((n_p, d), F32), jax.ShapeDtypeStruct((n_s, d), F32),
                   jax.ShapeDtypeStruct((n_p, d), F32), jax.ShapeDtypeStruct((n_s, d), F32),
                   jax.ShapeDtypeStruct((n_t, h), F32)),
        grid=(n_t // tm,),
        in_specs=[_rows_a(tm, d, nblk_p), _rows_b(tm, d, nblk_p, 0),
                  _full((d, wc)), _full((d, h)), _full((1, h))],
        out_specs=(_rows(tm, 3 * d),
                   _rows_a(tm, d, nblk_p), _rows_b(tm, d, nblk_p, 0),
                   _rows_a(tm, d, nblk_p), _rows_b(tm, d, nblk_p, 0),
                   _rows(tm, h)),
        compiler_params=_cparams(1, 48),
        name="fox_proj",
    )(xp, xs, w_cat, wfh, b_f.reshape(1, h))


def _fox_aug_kernel(chunk, logf_ref, place_ref, ones_ref, qaug_ref, kaug_ref):
    t, h = logf_ref.shape
    r = lax.broadcasted_iota(I32, (chunk, chunk), 0)
    c = lax.broadcasted_iota(I32, (chunk, chunk), 1)
    tri = jnp.where(c <= r, 1.0, 0.0).astype(BF16)
    carry = jnp.zeros((1, h), F32)
    for j in range(t // chunk):
        rows = pl.ds(j * chunk, chunk)
        hi, mid, lo = _split3(logf_ref[rows, :])
        cs = _dot(tri, hi) + _dot(tri, mid) + _dot(tri, lo) + carry
        carry = cs[chunk - 1:chunk, :]
        chi, cmid, clo = _split3(cs)
        qa = _dot(chi, place_ref[0]) + _dot(cmid, place_ref[1]) + _dot(clo, place_ref[2]) + ones_ref[0]
        ka = ones_ref[1] - (_dot(chi, place_ref[3]) + _dot(cmid, place_ref[4]) + _dot(clo, place_ref[5]))
        qaug_ref[rows, :] = qa.astype(BF16)
        kaug_ref[rows, :] = ka.astype(BF16)


def _fox_aug(logf3):
    b, t, h = logf3.shape
    assert h * AUG_COLS <= LANES
    chunk = 512 if t % 512 == 0 else LANES
    assert t % chunk == 0
    place = np.zeros((AUG_COLS, h, LANES), np.float32)
    ones = np.zeros((2, 1, LANES), np.float32)
    for hh in range(h):
        for j in range(AUG_COLS):
            place[j, hh, AUG_COLS * hh + j] = 1.0
        ones[0, 0, AUG_COLS * hh + 3:AUG_COLS * hh + 6] = 1.0
        ones[1, 0, AUG_COLS * hh:AUG_COLS * hh + 3] = 1.0
    return pl.pallas_call(
        functools.partial(_fox_aug_kernel, chunk),
        out_shape=(jax.ShapeDtypeStruct((b, t, LANES), BF16), jax.ShapeDtypeStruct((b, t, LANES), BF16)),
        grid=(b,),
        in_specs=[pl.BlockSpec((None, t, h), lambda i: (i, 0, 0)),
                  _full((AUG_COLS, h, LANES)), _full((2, 1, LANES))],
        out_specs=(pl.BlockSpec((None, t, LANES), lambda i: (i, 0, 0)),
                   pl.BlockSpec((None, t, LANES), lambda i: (i, 0, 0))),
        compiler_params=_cparams(1, 32),
        name="fox_aug",
    )(logf3, jnp.asarray(place, BF16), jnp.asarray(ones, F32))


def _attn_kernel(fox, tq, tk, q_off, nq, *refs):
    if fox:
        q_ref, k_ref, v_ref, qaug_ref, kaug_ref, o_ref = refs
    else:
        q_ref, k_ref, v_ref, o_ref = refs
    pair = pl.program_id(1)
    iq = pl.program_id(2)
    half = LANES // 2
    sub = min(tq, ATTN_SUB)
    lane = lax.broadcasted_iota(I32, (1, LANES), 1)
    lo_half = lane < half
    qops = []
    for hh in range(2):
        if fox:
            head = 2 * pair + hh
            amask = (lane >= AUG_COLS * head) & (lane < AUG_COLS * head + AUG_COLS)
            scale = half ** -0.5
            qops.append(jnp.concatenate(
                [jnp.where(lo_half == (hh == 0), q_ref[...], 0.0).astype(BF16) * jnp.asarray(scale, BF16),
                 jnp.where(amask, qaug_ref[...], 0.0).astype(BF16)], axis=1))
        else:
            qops.append(q_ref[:, hh * LANES:(hh + 1) * LANES])

    ones_lo = jnp.where(lo_half, 1.0, 0.0).astype(BF16)
    ones_hi = jnp.where(lo_half, 0.0, 1.0).astype(BF16)

    def attend(iq_s):
        q_first = q_off + iq_s * tq
        kf = (q_first // tk) * tk
        for r in reversed(range(tq // sub)):
            rsl = slice(r * sub, (r + 1) * sub)
            nd = (r + 1) * sub if tq == tk else tk
            nk = kf + nd
            qpos = q_first + r * sub + lax.broadcasted_iota(I32, (sub, nd), 0)
            kpos = kf + lax.broadcasted_iota(I32, (sub, nd), 1)
            if fox:
                allowed = kpos <= qpos
            else:
                allowed = (lax.shift_right_logical(kpos, CHUNK_LOG2)
                           <= lax.shift_right_logical(qpos, CHUNK_LOG2))
            ps = []
            for hh in range(2):
                if fox:
                    kop = jnp.concatenate([k_ref[:nk, :], kaug_ref[:nk, :]], axis=1)
                else:
                    kop = k_ref[:nk, hh * LANES:(hh + 1) * LANES]
                s = lax.dot_general(qops[hh][rsl], kop, (((1,), (1,)), ((), ())), preferred_element_type=F32)
                s_last = jnp.where(allowed, s[:, kf:], NEG_INF)
                s = s_last if kf == 0 else jnp.concatenate([s[:, :kf], s_last], axis=1)
                m = jnp.max(s, axis=1, keepdims=True)
                ps.append(jnp.exp(s - m).astype(BF16))
            v = v_ref[:nk, :]
            vop = jnp.concatenate(
                [jnp.concatenate([jnp.where(lo_half, v, 0.0).astype(BF16), jnp.broadcast_to(ones_lo, v.shape)], axis=1),
                 jnp.concatenate([jnp.where(lo_half, 0.0, v).astype(BF16), jnp.broadcast_to(ones_hi, v.shape)], axis=1)],
                axis=0)
            acc = _dot(jnp.concatenate(ps, axis=1), vop)
            o_ref[rsl, :] = (acc[:, :LANES] / acc[:, LANES:]).astype(o_ref.dtype)

    for iq_s in range(nq):
        pl.when(iq == iq_s)(functools.partial(attend, iq_s))


def _attention(fox, q_arr, k_arr, v_arr, aug, *, n_batch, n_pairs, t_q, t_k, tq, tk, q_off,
               q_row0, q_col0, k_col0, v_col0, out_rows, name):
    assert t_q % tq == 0 and t_k % tk == 0 and q_off % tq == 0
    assert (tq == tk and q_off % tk == 0) or tk == t_k
    nq = t_q // tq
    qw = LANES if fox else 2 * LANES
    qb0 = q_row0 // tq
    assert q_row0 % tq == 0
    in_specs = [
        pl.BlockSpec((tq, qw), lambda b, p, i: (qb0 + b * nq + i, q_col0 + p)),
        pl.BlockSpec((t_k, qw), lambda b, p, i: (b, k_col0 + p)),
        pl.BlockSpec((t_k, LANES), lambda b, p, i: (b, v_col0 + p)),
    ]
    args = [q_arr, k_arr, v_arr]
    if fox:
        qaug, kaug = aug
        ab0 = q_off // tq
        in_specs += [pl.BlockSpec((None, tq, LANES), lambda b, p, i: (b, ab0 + i, 0)),
                     pl.BlockSpec((None, t_k, LANES), lambda b, p, i: (b, 0, 0))]
        args += [qaug, kaug]
    return pl.pallas_call(
        functools.partial(_attn_kernel, fox, tq, tk, q_off, nq),
        out_shape=jax.ShapeDtypeStruct((out_rows, n_pairs * LANES), BF16),
        grid=(n_batch, n_pairs, nq),
        in_specs=in_specs,
        out_specs=pl.BlockSpec((tq, LANES), lambda b, p, i: (b * nq + i, p)),
        compiler_params=_cparams(3, 48),
        name=name,
    )(*args)


def _to_row_tiles(ref, val):
    rows = val.shape[0]
    for c in range(SUBLANES):
        ref[pl.ds(c, rows, stride=SUBLANES), :] = val[:, c * LANES:(c + 1) * LANES]


def _from_row_tiles(ref):
    rows = ref.shape[0] // SUBLANES
    return jnp.concatenate([ref[pl.ds(c, rows, stride=SUBLANES), :] for c in range(SUBLANES)], axis=1)


def _proj_ln_kernel(nblk_p, oa_ref, ob_ref, xa_ref, xb_ref, w_ref, g_ref, b_ref, out_ref, out3_ref):
    is_p = pl.program_id(0) < nblk_p
    o = jnp.where(is_p, oa_ref[...], ob_ref[...])
    x = jnp.where(is_p, xa_ref[...], xb_ref[...])
    z = DEEPNORM_ALPHA * x + _dot(o, w_ref[...])
    out = _layernorm(z, g_ref[...], b_ref[...])
    out_ref[...] = out
    _to_row_tiles(out3_ref, out)


def _proj_ln(o_p, o_s, x_pair, w_o, g, b):
    n_p, d = o_p.shape
    n_s = o_s.shape[0]
    n_t = n_p + n_s
    tm = ROW_TILE
    nblk_p = n_p // tm
    if isinstance(x_pair, tuple):
        xa, xb = x_pair
        xb_spec = _rows_b(tm, d, nblk_p, 0)
    else:
        xa = xb = x_pair
        xb_spec = _rows_b(tm, d, nblk_p, nblk_p)
    return pl.pallas_call(
        functools.partial(_proj_ln_kernel, nblk_p),
        out_shape=(jax.ShapeDtypeStruct((n_t, d), F32),
                   jax.ShapeDtypeStruct((n_t * SUBLANES, LANES), F32)),
        grid=(n_t // tm,),
        in_specs=[_rows_a(tm, d, nblk_p), _rows_b(tm, d, nblk_p, 0),
                  _rows_a(tm, d, nblk_p), xb_spec,
                  _full((d, d)), _full((1, d)), _full((1, d))],
        out_specs=(_rows(tm, d), _rows(tm * SUBLANES, LANES)),
        compiler_params=_cparams(1, 32),
        name="proj_ln",
    )(o_p, o_s, xa, xb, w_o.astype(BF16), g.reshape(1, d), b.reshape(1, d))


def _router_kernel(x_ref, wc_ref, b_ref, ti_ref, tg_ref, cnt_ref):
    xh, xl = _split2(x_ref[...])
    a = _dot(xh, wc_ref[...])
    logits = a[:, :LANES] + a[:, LANES:] + _dot(xl, wc_ref[:, :LANES]) + b_ref[...]
    lane = lax.broadcasted_iota(I32, logits.shape, 1).astype(F32)
    vals, idxs = [], []
    cur = logits
    for _ in range(TOP_K):
        m = jnp.max(cur, axis=1, keepdims=True)
        idx = jnp.min(jnp.where(cur == m, lane, float(LANES)), axis=1, keepdims=True)
        vals.append(m)
        idxs.append(idx)
        cur = jnp.where(lane == idx, -jnp.inf, cur)
    exps = [jnp.exp(v - vals[0]) for v in vals]
    denom = exps[0]
    for e in exps[1:]:
        denom = denom + e
    ti = jnp.zeros(logits.shape, F32)
    tg = jnp.zeros(logits.shape, F32)
    for k in range(TOP_K):
        ti = jnp.where(lane == float(k), idxs[k], ti)
        tg = jnp.where(lane == float(k), exps[k] / denom, tg)
    ti_ref[...] = ti.astype(I32)
    tg_ref[...] = tg

    hot = jnp.zeros(logits.shape, F32)
    for k in range(TOP_K):
        hot = hot + jnp.where(lane == idxs[k], 1.0, 0.0)

    @pl.when(pl.program_id(0) == 0)
    def _():
        cnt_ref[...] = jnp.zeros(cnt_ref.shape, F32)

    cnt_ref[...] += jnp.sum(hot, axis=0, keepdims=True)


def _router(x, w_r, b_r):
    n_t, d = x.shape
    e = w_r.shape[-1]
    tm = 2 * ROW_TILE if n_t % (2 * ROW_TILE) == 0 else ROW_TILE
    w_pad = jnp.zeros((d, LANES), F32).at[:, :e].set(w_r)
    b_pad = jnp.full((1, LANES), -jnp.inf, F32).at[0, :e].set(b_r)
    wh, wl = _split2(w_pad)
    return pl.pallas_call(
        _router_kernel,
        out_shape=(jax.ShapeDtypeStruct((n_t, LANES), I32), jax.ShapeDtypeStruct((n_t, LANES), F32),
                   jax.ShapeDtypeStruct((8, LANES), F32)),
        grid=(n_t // tm,),
        in_specs=[_rows(tm, d), _full((d, 2 * LANES)), _full((1, LANES))],
        out_specs=(_rows(tm, LANES), _rows(tm, LANES), _full((8, LANES))),
        compiler_params=_cparams(1, 32),
        name="router",
    )(x, jnp.concatenate([wh, wl], axis=1), b_pad)


def _moe_plan(top_i, counts):
    n = top_i.shape[0]
    n_experts = counts.shape[0]
    blk = MOE_BLOCK
    p = n * TOP_K
    assert p % blk == 0
    bits = max(p - 1, 1).bit_length()
    assert n_experts << bits < 2 ** 31
    packed = top_i.reshape(p) * (1 << bits) + jnp.arange(p, dtype=I32)
    order = jnp.sort(packed) & ((1 << bits) - 1)
    starts = jnp.cumsum(counts) - counts
    nblk_e = (counts + blk - 1) // blk
    bends = jnp.cumsum(nblk_e)
    n_blocks = p // blk + n_experts
    blk_ids = jnp.arange(n_blocks, dtype=I32)
    block_e = jnp.minimum(jnp.sum((bends[None, :] <= blk_ids[:, None]).astype(I32), axis=1), n_experts - 1)
    b_first = (blk_ids - (bends - nblk_e)[block_e]) * blk
    b_cnt = counts[block_e]
    b_start = starts[block_e]
    r_in = jnp.arange(blk, dtype=I32)[None, :]
    off = b_first[:, None] + r_in
    valid = off < b_cnt[:, None]
    pair = order[jnp.clip(b_start[:, None] + off, 0, p - 1)]
    row = blk_ids[:, None] * blk + r_in
    pad_idx = row - (b_start + b_cnt)[:, None]
    src_tok = jnp.where(valid, pair // TOP_K, 0)
    dst_row = jnp.where(valid, (pair % TOP_K) * n + pair // TOP_K, p + pad_idx)
    return block_e, src_tok.reshape(n_blocks, 1, blk), dst_row.reshape(n_blocks, 1, blk)


def _moe_kernel(blk, nb, be_ref, src_ref, src1_ref, src2_ref, dstp_ref, x_hbm, wgu_ref, bgu_ref, wdn_ref, bdn_ref,
                y_hbm, xb0, xb1, xb2, yb0, yb1, yb2, wgu16, wdn16, gsem, ssem):
    i = pl.program_id(0)
    ring = MOE_RING
    spare0 = nb * blk

    xbuf = (xb0, xb1, xb2)
    ybuf = (yb0, yb1, yb2)

    def gather_copy(tok, s, r):
        return pltpu.make_async_copy(x_hbm.at[tok], xbuf[s].at[pl.ds(r * SUBLANES, SUBLANES)], gsem.at[s])

    def scatter_copy(row, s, r):
        return pltpu.make_async_copy(ybuf[s].at[pl.ds(r * SUBLANES, SUBLANES)], y_hbm.at[row], ssem.at[s])

    @pl.when(i == 0)
    def _():
        for yb in ybuf:
            yb[...] = jnp.zeros(yb.shape, F32)
        for r in range(blk):
            gather_copy(src_ref[0, r], 0, r).start()
        for r in range(blk):
            gather_copy(src1_ref[0, r], 1, r).start()
        for s in range(2):
            for r in range(blk):
                scatter_copy(spare0 + s * blk + r, s, r).start()

    changed = jnp.logical_or(i == 0, be_ref[jnp.minimum(i, nb - 1)] != be_ref[jnp.clip(i - 1, 0, nb - 1)])

    @pl.when(jnp.logical_and(i < nb, changed))
    def _():
        wgu16[...] = wgu_ref[...].astype(BF16)
        wdn16[...] = wdn_ref[...].astype(BF16)

    def block_step(slot):
        far = (slot + 2) % ring
        for r in range(blk):
            gather_copy(0, slot, r).wait()
        for r in range(blk):
            scatter_copy(0, slot, r).wait()
        for r in range(blk):
            gather_copy(src2_ref[0, r], far, r).start(priority=r % 2)
        for r in range(blk):
            scatter_copy(dstp_ref[0, r], far, r).start(priority=r % 2)
        f = wdn16.shape[0]
        x = _from_row_tiles(xbuf[slot]).astype(BF16)
        gu = _dot(x, wgu16[...]) + bgu_ref[...]
        g = jnp.minimum(gu[:, :f], SWIGLU_LIMIT)
        u = jnp.clip(gu[:, f:], -SWIGLU_LIMIT, SWIGLU_LIMIT)
        hid = g * jax.nn.sigmoid(SWIGLU_ALPHA * g) * (u + 1.0)
        _to_row_tiles(ybuf[slot], _dot(hid.astype(BF16), wdn16[...]) + bdn_ref[...])

    for slot in range(ring):
        pl.when(jnp.logical_and(i < nb, i % ring == slot))(functools.partial(block_step, slot))

    @pl.when(i == nb)
    def _():
        for s in (nb % ring, (nb + 1) % ring):
            for r in range(blk):
                gather_copy(0, s, r).wait()
            for r in range(blk):
                scatter_copy(0, s, r).wait()
        last = (nb + 2) % ring
        for r in range(blk):
            scatter_copy(dstp_ref[0, r], last, r).start()
        for r in range(blk):
            scatter_copy(0, last, r).wait()


def _moe(x3, layer, block_e, src_tok, dst_row, w_gu, b_gu, w_dn, b_dn):
    _, n_e, d, f2 = w_gu.shape
    assert x3.shape[1:] == (SUBLANES, LANES) and d == SUBLANES * LANES
    f = f2 // 2
    blk = MOE_BLOCK
    nb = src_tok.shape[0]
    assert nb >= MOE_RING
    last = nb - 1
    spare2 = (nb + 2) * blk + jnp.arange(blk, dtype=I32).reshape(1, 1, blk)
    dst_late = jnp.concatenate([spare2, dst_row], axis=0)
    grid_spec = pltpu.PrefetchScalarGridSpec(
        num_scalar_prefetch=1,
        grid=(nb + 1,),
        in_specs=[
            pl.BlockSpec((None, 1, blk), lambda i, be: (jnp.minimum(i, last), 0, 0), memory_space=pltpu.SMEM),
            pl.BlockSpec((None, 1, blk), lambda i, be: (jnp.minimum(i + 1, last), 0, 0), memory_space=pltpu.SMEM),
            pl.BlockSpec((None, 1, blk), lambda i, be: (jnp.minimum(i + 2, last), 0, 0), memory_space=pltpu.SMEM),
            pl.BlockSpec((None, 1, blk), lambda i, be: (i, 0, 0), memory_space=pltpu.SMEM),
            pl.BlockSpec(memory_space=pl.ANY),
            pl.BlockSpec((None, None, d, f2), lambda i, be: (layer, be[jnp.minimum(i, last)], 0, 0)),
            pl.BlockSpec((None, None, 1, f2), lambda i, be: (layer, be[jnp.minimum(i, last)], 0, 0)),
            pl.BlockSpec((None, None, f, d), lambda i, be: (layer, be[jnp.minimum(i, last)], 0, 0)),
            pl.BlockSpec((None, None, 1, d), lambda i, be: (layer, be[jnp.minimum(i, last)], 0, 0)),
        ],
        out_specs=pl.BlockSpec(memory_space=pl.ANY),
        scratch_shapes=[pltpu.VMEM((blk * SUBLANES, LANES), F32)] * (2 * MOE_RING) + [
                        pltpu.VMEM((d, f2), BF16), pltpu.VMEM((f, d), BF16),
                        pltpu.SemaphoreType.DMA((MOE_RING,)), pltpu.SemaphoreType.DMA((MOE_RING,))],
    )
    return pl.pallas_call(
        functools.partial(_moe_kernel, blk, nb),
        out_shape=jax.ShapeDtypeStruct(((nb + MOE_RING) * blk, SUBLANES, LANES), F32),
        grid_spec=grid_spec,
        compiler_params=_cparams(1, 56),
        name="moe_experts",
    )(block_e, src_tok, src_tok, src_tok, dst_late, x3, w_gu, b_gu.reshape(b_gu.shape[0], n_e, 1, f2),
      w_dn, b_dn.reshape(b_dn.shape[0], n_e, 1, d))


def _combine_ln_kernel(nblk_p, split, *refs):
    y_refs = refs[:TOP_K]
    tg_ref, x_ref, g_ref, b_ref = refs[TOP_K:TOP_K + 4]
    out_refs = refs[TOP_K + 4:]
    tg = tg_ref[...]
    y = tg[:, 0:1] * _from_row_tiles(y_refs[0])
    for k in range(1, TOP_K):
        y = y + tg[:, k:k + 1] * _from_row_tiles(y_refs[k])
    out = _layernorm(DEEPNORM_ALPHA * x_ref[...] + y, g_ref[...], b_ref[...])
    if split:
        is_p = pl.program_id(0) < nblk_p

        @pl.when(is_p)
        def _():
            out_refs[0][...] = out

        @pl.when(jnp.logical_not(is_p))
        def _():
            out_refs[1][...] = out
    else:
        out_refs[0][...] = out


def _combine_ln(y_rows, tg, x, g, b, n_p, split):
    n_t, d = x.shape
    tm = ROW_TILE
    nblk_p = n_p // tm
    nblk = n_t // tm
    y_rows = y_rows.reshape(-1, LANES)
    y_specs = [pl.BlockSpec((tm * SUBLANES, LANES), functools.partial(lambda k, i: (k * nblk + i, 0), k))
               for k in range(TOP_K)]
    if split:
        out_shape = (jax.ShapeDtypeStruct((n_p, d), F32), jax.ShapeDtypeStruct((n_t - n_p, d), F32))
        out_specs = (_rows_a(tm, d, nblk_p), _rows_b(tm, d, nblk_p, 0))
    else:
        out_shape = jax.ShapeDtypeStruct((n_t, d), F32)
        out_specs = _rows(tm, d)
    return pl.pallas_call(
        functools.partial(_combine_ln_kernel, nblk_p, split),
        out_shape=out_shape,
        grid=(n_t // tm,),
        in_specs=y_specs + [_rows(tm, LANES), _rows(tm, d), _full((1, d)), _full((1, d))],
        out_specs=out_specs,
        compiler_params=_cparams(1, 32),
        name="combine_ln",
    )(*([y_rows] * TOP_K), tg, x, g.reshape(1, d), b.reshape(1, d))


def _moe_layer(x, x3, layer, n_p, split, ln_g, ln_b, w_r, b_r, w_gu, b_gu, w_dn, b_dn):
    n_e = w_r.shape[-1]
    ti, tg, cnt = _router(x, w_r[layer], b_r[layer])
    block_e, src_tok, dst_row = _moe_plan(ti[:, :TOP_K], cnt[0, :n_e].astype(I32))
    y_rows = _moe(x3.reshape(-1, SUBLANES, LANES), layer, block_e, src_tok, dst_row, w_gu, b_gu, w_dn, b_dn)
    return _combine_ln(y_rows, tg, x, ln_g, ln_b, n_p, split)


def _mla_proj_kernel(nblk_p, q_lora, kv_lora, n_heads, x_ref, w1_ref, gq_ref, w2a_ref, w2b_ref, gkv_ref,
                     qc_ref, qs_ref, kc_ref, ks_ref,
                     q_ref, ckvp_ref, ckvs_ref, kpep_ref, kpes_ref, ckv16_ref, kpe16_ref):
    is_p = pl.program_id(0) < nblk_p
    x = x_ref[...].astype(BF16)
    y1 = _dot(x, w1_ref[...])
    a = _rmsnorm(y1[:, :q_lora], gq_ref[...]).astype(BF16)
    qa = _dot(a, w2a_ref[...])
    qb = _dot(a, w2b_ref[...])
    qc = qc_ref[...]
    qs = qs_ref[...]
    for h in range(n_heads):
        cols = slice(h * LANES, (h + 1) * LANES)
        q_ref[:, cols] = (qa[:, cols] * qc + qb[:, cols] * qs).astype(BF16)
    ckv = _rmsnorm(y1[:, q_lora:q_lora + kv_lora], gkv_ref[...])
    c0 = q_lora + kv_lora
    kpe = y1[:, c0:c0 + LANES] * kc_ref[...] + y1[:, c0 + LANES:c0 + 2 * LANES] * ks_ref[...]
    ckv16_ref[...] = ckv.astype(BF16)
    kpe16_ref[...] = kpe.astype(BF16)
    rope = kpep_ref.shape[-1]

    @pl.when(is_p)
    def _():
        ckvp_ref[...] = ckv
        kpep_ref[...] = kpe[:, :rope]

    @pl.when(jnp.logical_not(is_p))
    def _():
        ckvs_ref[...] = ckv
        kpes_ref[...] = kpe[:, :rope]


def _rot_cols(w):
    half = w.shape[-1] // 2
    return jnp.concatenate([-w[..., half:], w[..., :half]], axis=-1)


def _mla_proj(x, n_p, t_p, t_s, past, w_dq, g_q, w_uq, w_dkv, g_kv):
    n_t, d = x.shape
    n_s = n_t - n_p
    tm = ROW_TILE
    nblk_p = n_p // tm
    q_lora = w_dq.shape[-1]
    kv_lora = g_kv.shape[-1]
    hd = MLA_NOPE + MLA_ROPE
    n_heads = w_uq.shape[-1] // hd
    assert q_lora % LANES == 0 and kv_lora % LANES == 0 and t_p % tm == 0 and tm % t_s == 0

    w_pe = w_dkv[:, kv_lora:]
    pad = jnp.zeros((d, LANES - MLA_ROPE), F32)
    w1 = jnp.concatenate([w_dq, w_dkv[:, :kv_lora], w_pe, pad, _rot_cols(w_pe), pad], axis=1).astype(BF16)

    w3 = w_uq.reshape(q_lora, n_heads, hd)
    zq = jnp.zeros((q_lora, n_heads, LANES - hd), F32)
    w2a = jnp.concatenate([w3, zq], axis=2).reshape(q_lora, n_heads * LANES).astype(BF16)
    w2b = jnp.concatenate([jnp.zeros((q_lora, n_heads, MLA_NOPE), F32), _rot_cols(w3[:, :, MLA_NOPE:]), zq],
                          axis=2).reshape(q_lora, n_heads * LANES).astype(BF16)

    half = MLA_ROPE // 2
    inv_freq = ROPE_THETA ** (-jnp.arange(half, dtype=F32) / half)
    pos = jnp.concatenate([jnp.arange(t_p), past + (jnp.arange(tm) % t_s)]).astype(F32)
    ang = pos[:, None] * inv_freq[None, :]
    cos2 = jnp.concatenate([jnp.cos(ang), jnp.cos(ang)], axis=1)
    sin2 = jnp.concatenate([jnp.sin(ang), jnp.sin(ang)], axis=1)
    rows = pos.shape[0]
    scale = hd ** -0.5
    z32 = jnp.zeros((rows, LANES - hd), F32)
    q_cos = jnp.concatenate([jnp.full((rows, MLA_NOPE), scale, F32), scale * cos2, z32], axis=1)
    q_sin = jnp.concatenate([jnp.zeros((rows, MLA_NOPE), F32), scale * sin2, z32], axis=1)
    z96 = jnp.zeros((rows, LANES - MLA_ROPE), F32)
    k_cos = jnp.concatenate([cos2, z96], axis=1)
    k_sin = jnp.concatenate([sin2, z96], axis=1)
    tpb = t_p // tm
    tbl = pl.BlockSpec((tm, LANES), lambda i: (jnp.where(i < nblk_p, i % tpb, tpb), 0))

    w1w = w1.shape[1]
    return pl.pallas_call(
        functools.partial(_mla_proj_kernel, nblk_p, q_lora, kv_lora, n_heads),
        out_shape=(jax.ShapeDtypeStruct((n_t, n_heads * LANES), BF16),
                   jax.ShapeDtypeStruct((n_p, kv_lora), F32), jax.ShapeDtypeStruct((n_s, kv_lora), F32),
                   jax.ShapeDtypeStruct((n_p, MLA_ROPE), F32), jax.ShapeDtypeStruct((n_s, MLA_ROPE), F32),
                   jax.ShapeDtypeStruct((n_t, kv_lora), BF16), jax.ShapeDtypeStruct((n_t, LANES), BF16)),
        grid=(n_t // tm,),
        in_specs=[_rows(tm, d), _full((d, w1w)), _full((1, q_lora)),
                  _full((q_lora, n_heads * LANES)), _full((q_lora, n_heads * LANES)), _full((1, kv_lora)),
                  tbl, tbl, tbl, tbl],
        out_specs=(_rows(tm, n_heads * LANES),
                   _rows_a(tm, kv_lora, nblk_p), _rows_b(tm, kv_lora, nblk_p, 0),
                   _rows_a(tm, MLA_ROPE, nblk_p), _rows_b(tm, MLA_ROPE, nblk_p, 0),
                   _rows(tm, kv_lora), _rows(tm, LANES)),
        compiler_params=_cparams(1, 48),
        name="mla_proj",
    )(x, w1, g_q.reshape(1, q_lora), w2a, w2b, g_kv.reshape(1, kv_lora), q_cos, q_sin, k_cos, k_sin)


def _mla_expand_kernel(ckv_ref, kpe_ref, wk_ref, pk_ref, wv_ref, k_ref, v_ref):
    ckv = ckv_ref[...]
    k_ref[...] = (_dot(ckv, wk_ref[...]) + _dot(kpe_ref[...], pk_ref[...])).astype(BF16)
    v_ref[...] = _dot(ckv, wv_ref[...]).astype(BF16)


def _mla_expand(ckv16, kpe16, n_rows, w_ukv):
    kv_lora = ckv16.shape[-1]
    hd = MLA_NOPE + MLA_V
    n_heads = w_ukv.shape[-1] // hd
    tm = next(t for t in (512, 256, LANES) if n_rows % t == 0)
    w3 = w_ukv.reshape(kv_lora, n_heads, hd)
    wk = jnp.concatenate([w3[:, :, :MLA_NOPE], jnp.zeros((kv_lora, n_heads, LANES - MLA_NOPE), F32)],
                         axis=2).reshape(kv_lora, n_heads * LANES).astype(BF16)
    wv = w3[:, :, MLA_NOPE:].reshape(kv_lora, n_heads * MLA_V).astype(BF16)
    pk = np.zeros((LANES, n_heads, LANES), np.float32)
    for j in range(MLA_ROPE):
        pk[j, :, MLA_NOPE + j] = 1.0
    pk = jnp.asarray(pk.reshape(LANES, n_heads * LANES), BF16)
    return pl.pallas_call(
        _mla_expand_kernel,
        out_shape=(jax.ShapeDtypeStruct((n_rows, n_heads * LANES), BF16),
                   jax.ShapeDtypeStruct((n_rows, n_heads * MLA_V), BF16)),
        grid=(n_rows // tm,),
        in_specs=[_rows(tm, kv_lora), _rows(tm, LANES), _full((kv_lora, n_heads * LANES)),
                  _full((LANES, n_heads * LANES)), _full((kv_lora, n_heads * MLA_V))],
        out_specs=(_rows(tm, n_heads * LANES), _rows(tm, n_heads * MLA_V)),
        compiler_params=_cparams(1, 32),
        name="mla_expand",
    )(ckv16, kpe16, wk, pk, wv)


def _pad_keys(cache, new, t_pad):
    b, t_c, w = cache.shape
    t_n = new.shape[1]
    parts = [cache.astype(BF16), new.astype(BF16)]
    if t_pad > t_c + t_n:
        parts.append(jnp.zeros((b, t_pad - t_c - t_n, w), BF16))
    return jnp.concatenate(parts, axis=1).reshape(b * t_pad, w)


def kernel(x_prompt, x_sample, cache_fox_k, cache_fox_v, cache_fox_logf, cache_mla_ckv, cache_mla_kpe, fox_w_qkv, fox_w_f, fox_b_f, fox_w_o, mla_w_dq, mla_g_q, mla_w_uq, mla_w_dkv, mla_g_kv, mla_w_ukv, mla_w_o, ln_g, ln_b, moe_w_router, moe_b_router, moe_w_gu, moe_b_gu, moe_w_down, moe_b_down):
    b_p, t_p, d = x_prompt.shape
    b_s, t_s, _ = x_sample.shape
    past = cache_fox_k.shape[2]
    n_p, n_s = b_p * t_p, b_s * t_s
    assert fox_w_qkv.shape[0] == 1 and mla_w_dq.shape[0] == 1 and ln_g.shape[0] == DEPTH
    tq_p = min(ATTN_TILE, t_p)
    assert n_p % ROW_TILE == 0 and n_s % ROW_TILE == 0 and t_p % tq_p == 0 and tq_p % ATTN_SUB == 0
    xp = x_prompt.reshape(n_p, d)
    xs = x_sample.reshape(n_s, d)
    moe_w = (moe_w_router, moe_b_router, moe_w_gu, moe_b_gu, moe_w_down, moe_b_down)
    t_all = -(-(past + t_s) // LANES) * LANES

    n_fh = fox_w_f.shape[-1]
    fd = d // n_fh
    assert 2 * fd == LANES
    qkv16, k_p, k_s, v_p, v_s, logf = _fox_proj(xp, xs, fox_w_qkv[0], fox_w_f[0], fox_b_f[0])
    logf_p = logf[:n_p].reshape(b_p, t_p, n_fh)
    logf_s = logf[n_p:].reshape(b_s, t_s, n_fh)
    qaug_p, kaug_p = _fox_aug(logf_p)
    logf_all = jnp.concatenate([cache_fox_logf[0].astype(F32), logf_s,
                                jnp.zeros((b_s, t_all - past - t_s, n_fh), F32)], axis=1)
    qaug_s, kaug_s = _fox_aug(logf_all)
    n_pairs = n_fh // 2
    o_p = _attention(True, qkv16, qkv16, qkv16, (qaug_p, kaug_p), n_batch=b_p, n_pairs=n_pairs,
                     t_q=t_p, t_k=t_p, tq=tq_p, tk=tq_p, q_off=0, q_row0=0,
                     q_col0=0, k_col0=n_pairs, v_col0=2 * n_pairs, out_rows=n_p, name="fox_attn_prompt")
    k_all = _pad_keys(cache_fox_k[0].reshape(b_s, past, d), k_s.reshape(b_s, t_s, d), t_all)
    v_all = _pad_keys(cache_fox_v[0].reshape(b_s, past, d), v_s.reshape(b_s, t_s, d), t_all)
    o_s = _attention(True, qkv16, k_all, v_all, (qaug_s, kaug_s), n_batch=b_s, n_pairs=n_pairs,
                     t_q=t_s, t_k=t_all, tq=t_s, tk=t_all, q_off=past, q_row0=n_p,
                     q_col0=0, k_col0=0, v_col0=0, out_rows=n_s, name="fox_attn_sample")
    x1, x1_rows = _proj_ln(o_p, o_s, (xp, xs), fox_w_o[0], ln_g[0, 0], ln_b[0, 0])
    x2 = _moe_layer(x1, x1_rows, 0, n_p, False, ln_g[0, 1], ln_b[0, 1], *moe_w)

    q16, ckv_p, ckv_s, kpe_p, kpe_s, ckv16, kpe16 = _mla_proj(
        x2, n_p, t_p, t_s, past, mla_w_dq[0], mla_g_q[0], mla_w_uq[0], mla_w_dkv[0], mla_g_kv[0])
    kv_lora = ckv_p.shape[-1]
    n_mh = mla_w_o.shape[1] // MLA_V
    m_pairs = n_mh // 2
    k16_p, v16_p = _mla_expand(ckv16, kpe16, n_p, mla_w_ukv[0])
    o_p = _attention(False, q16, k16_p, v16_p, None, n_batch=b_p, n_pairs=m_pairs,
                     t_q=t_p, t_k=t_p, tq=tq_p, tk=tq_p, q_off=0, q_row0=0,
                     q_col0=0, k_col0=0, v_col0=0, out_rows=n_p, name="mla_attn_prompt")
    ckv_all = _pad_keys(cache_mla_ckv[0], ckv16[n_p:].reshape(b_s, t_s, kv_lora), t_all)
    kpe_cache = jnp.pad(cache_mla_kpe[0], ((0, 0), (0, 0), (0, LANES - MLA_ROPE)))
    kpe_all = _pad_keys(kpe_cache, kpe16[n_p:].reshape(b_s, t_s, LANES), t_all)
    k16_s, v16_s = _mla_expand(ckv_all, kpe_all, b_s * t_all, mla_w_ukv[0])
    o_s = _attention(False, q16, k16_s, v16_s, None, n_batch=b_s, n_pairs=m_pairs,
                     t_q=t_s, t_k=t_all, tq=t_s, tk=t_all, q_off=past, q_row0=n_p,
                     q_col0=0, k_col0=0, v_col0=0, out_rows=n_s, name="mla_attn_sample")
    x3, x3_rows = _proj_ln(o_p, o_s, x2, mla_w_o[0], ln_g[1, 0], ln_b[1, 0])
    y_p, y_s = _moe_layer(x3, x3_rows, 1, n_p, True, ln_g[1, 1], ln_b[1, 1], *moe_w)

    return (y_p.reshape(b_p, t_p, d), y_s.reshape(b_s, t_s, d),
            k_p.reshape(1, b_p, t_p, n_fh, fd), v_p.reshape(1, b_p, t_p, n_fh, fd),
            logf_p.reshape(1, b_p, t_p, n_fh),
            ckv_p.reshape(1, b_p, t_p, kv_lora), kpe_p.reshape(1, b_p, t_p, MLA_ROPE),
            k_s.reshape(1, b_s, t_s, n_fh, fd), v_s.reshape(1, b_s, t_s, n_fh, fd),
            logf_s.reshape(1, b_s, t_s, n_fh),
            ckv_s.reshape(1, b_s, t_s, kv_lora), kpe_s.reshape(1, b_s, t_s, MLA_ROPE))
```

```python
import functools

import numpy as np
import jax
import jax.numpy as jnp
from jax import lax
from jax.experimental import pallas as pl
from jax.experimental.pallas import tpu as pltpu

F32 = jnp.float32
BF16 = jnp.bfloat16
I32 = jnp.int32

DEPTH = 2
CHUNK = 64
CHUNK_LOG2 = 6
assert 1 << CHUNK_LOG2 == CHUNK
MLA_NOPE = 64
MLA_ROPE = 32
MLA_V = 64
ROPE_THETA = 10000.0
TOP_K = 4
SWIGLU_LIMIT = 7.0
SWIGLU_ALPHA = 1.702
DEEPNORM_ALPHA = (2.0 * DEPTH) ** 0.25
LN_EPS = 1e-5
RMS_EPS = 1e-6
NEG_INF = -1e30

LANES = 128
SUBLANES = 8
AUG_COLS = 6
MIB = 1 << 20

ROW_TILE = 256
MOE_BLOCK = 256
ATTN_TILE = 2048
ATTN_SUB = 256
MOE_RING = 4


def _cparams(n_axes, vmem_mib):
    return pltpu.CompilerParams(dimension_semantics=("arbitrary",) * n_axes,
                                vmem_limit_bytes=vmem_mib * MIB)


def _full(shape):
    nd = len(shape)
    return pl.BlockSpec(shape, lambda *_: (0,) * nd)


def _rows(tm, width):
    return pl.BlockSpec((tm, width), lambda i: (i, 0))


def _rows_a(tm, width, nblk_a):
    return pl.BlockSpec((tm, width), lambda i: (jnp.minimum(i, nblk_a - 1), 0))


def _rows_b(tm, width, nblk_a, base):
    return pl.BlockSpec((tm, width), lambda i: (base + jnp.maximum(i - nblk_a, 0), 0))


def _split2(x):
    hi = x.astype(BF16)
    lo = (x - hi.astype(F32)).astype(BF16)
    return hi, lo


def _split3(x):
    hi = x.astype(BF16)
    r = x - hi.astype(F32)
    mid = r.astype(BF16)
    lo = (r - mid.astype(F32)).astype(BF16)
    return hi, mid, lo


def _dot(a, b):
    return jnp.dot(a, b, preferred_element_type=F32)


def _layernorm(z, g, b):
    mu = jnp.mean(z, axis=-1, keepdims=True)
    zc = z - mu
    var = jnp.mean(zc * zc, axis=-1, keepdims=True)
    return zc * lax.rsqrt(var + LN_EPS) * g + b


def _rmsnorm(x, g):
    return x * lax.rsqrt(jnp.mean(x * x, axis=-1, keepdims=True) + RMS_EPS) * g


def _fox_proj_kernel(nblk_p, xa_ref, xb_ref, wqkv_ref, wfh_ref, bf_ref,
                     qkv_ref, kp_ref, ks_ref, vp_ref, vs_ref, logf_ref):
    i = pl.program_id(0)
    is_p = i < nblk_p
    x = jnp.where(is_p, xa_ref[...], xb_ref[...])
    xh, xl = _split2(x)
    y = _dot(xh, wqkv_ref[...])
    d = kp_ref.shape[-1]
    h = logf_ref.shape[-1]
    qkv = y[:, :3 * d]
    qkv_ref[...] = qkv.astype(BF16)
    k = qkv[:, d:2 * d]
    v = qkv[:, 2 * d:3 * d]

    @pl.when(is_p)
    def _():
        kp_ref[...] = k
        vp_ref[...] = v

    @pl.when(jnp.logical_not(is_p))
    def _():
        ks_ref[...] = k
        vs_ref[...] = v

    z = (y[:, 3 * d:3 * d + h] + y[:, 3 * d + LANES:3 * d + LANES + h]
         + _dot(xl, wfh_ref[...]) + bf_ref[...])
    logf_ref[...] = -(jnp.maximum(-z, 0.0) + jnp.log1p(jnp.exp(-jnp.abs(z))))


def _fox_proj(xp, xs, w_qkv, w_f, b_f):
    n_p, d = xp.shape
    n_s = xs.shape[0]
    n_t = n_p + n_s
    h = w_f.shape[-1]
    tm = ROW_TILE
    nblk_p = n_p // tm
    wfh, wfl = _split2(w_f)
    pad = jnp.zeros((d, LANES - h), BF16)
    w_cat = jnp.concatenate([w_qkv.astype(BF16), wfh, pad, wfl, pad], axis=1)
    wc = w_cat.shape[1]
    return pl.pallas_call(
        functools.partial(_fox_proj_kernel, nblk_p),
        out_shape=(jax.ShapeDtypeStruct((n_t, 3 * d), BF16),
                   jax.ShapeDtypeStruct((n_p, d), F32), jax.ShapeDtypeStruct((n_s, d), F32),
                   jax.ShapeDtypeStruct((n_p, d), F32), jax.ShapeDtypeStruct((n_s, d), F32),
                   jax.ShapeDtypeStruct((n_t, h), F32)),
        grid=(n_t // tm,),
        in_specs=[_rows_a(tm, d, nblk_p), _rows_b(tm, d, nblk_p, 0),
                  _full((d, wc)), _full((d, h)), _full((1, h))],
        out_specs=(_rows(tm, 3 * d),
                   _rows_a(tm, d, nblk_p), _rows_b(tm, d, nblk_p, 0),
                   _rows_a(tm, d, nblk_p), _rows_b(tm, d, nblk_p, 0),
                   _rows(tm, h)),
        compiler_params=_cparams(1, 48),
        name="fox_proj",
    )(xp, xs, w_cat, wfh, b_f.reshape(1, h))


def _fox_aug_kernel(chunk, logf_ref, place_ref, ones_ref, qaug_ref, kaug_ref):
    t, h = logf_ref.shape
    r = lax.broadcasted_iota(I32, (chunk, chunk), 0)
    c = lax.broadcasted_iota(I32, (chunk, chunk), 1)
    tri = jnp.where(c <= r, 1.0, 0.0).astype(BF16)
    carry = jnp.zeros((1, h), F32)
    for j in range(t // chunk):
        rows = pl.ds(j * chunk, chunk)
        hi, mid, lo = _split3(logf_ref[rows, :])
        cs = _dot(tri, hi) + _dot(tri, mid) + _dot(tri, lo) + carry
        carry = cs[chunk - 1:chunk, :]
        chi, cmid, clo = _split3(cs)
        qa = _dot(chi, place_ref[0]) + _dot(cmid, place_ref[1]) + _dot(clo, place_ref[2]) + ones_ref[0]
        ka = ones_ref[1] - (_dot(chi, place_ref[3]) + _dot(cmid, place_ref[4]) + _dot(clo, place_ref[5]))
        qaug_ref[rows, :] = qa.astype(BF16)
        kaug_ref[rows, :] = ka.astype(BF16)


def _fox_aug(logf3):
    b, t, h = logf3.shape
    assert h * AUG_COLS <= LANES
    chunk = 512 if t % 512 == 0 else LANES
    assert t % chunk == 0
    place = np.zeros((AUG_COLS, h, LANES), np.float32)
    ones = np.zeros((2, 1, LANES), np.float32)
    for hh in range(h):
        for j in range(AUG_COLS):
            place[j, hh, AUG_COLS * hh + j] = 1.0
        ones[0, 0, AUG_COLS * hh + 3:AUG_COLS * hh + 6] = 1.0
        ones[1, 0, AUG_COLS * hh:AUG_COLS * hh + 3] = 1.0
    return pl.pallas_call(
        functools.partial(_fox_aug_kernel, chunk),
        out_shape=(jax.ShapeDtypeStruct((b, t, LANES), BF16), jax.ShapeDtypeStruct((b, t, LANES), BF16)),
        grid=(b,),
        in_specs=[pl.BlockSpec((None, t, h), lambda i: (i, 0, 0)),
                  _full((AUG_COLS, h, LANES)), _full((2, 1, LANES))],
        out_specs=(pl.BlockSpec((None, t, LANES), lambda i: (i, 0, 0)),
                   pl.BlockSpec((None, t, LANES), lambda i: (i, 0, 0))),
        compiler_params=_cparams(1, 32),
        name="fox_aug",
    )(logf3, jnp.asarray(place, BF16), jnp.asarray(ones, F32))


def _attn_kernel(fox, tq, tk, q_off, nq, *refs):
    if fox:
        q_ref, k_ref, v_ref, qaug_ref, kaug_ref, o_ref = refs
    else:
        q_ref, k_ref, v_ref, o_ref = refs
    pair = pl.program_id(1)
    iq = pl.program_id(2)
    half = LANES // 2
    sub = min(tq, ATTN_SUB)
    lane = lax.broadcasted_iota(I32, (1, LANES), 1)
    lo_half = lane < half
    qops = []
    for hh in range(2):
        if fox:
            head = 2 * pair + hh
            amask = (lane >= AUG_COLS * head) & (lane < AUG_COLS * head + AUG_COLS)
            scale = half ** -0.5
            qops.append(jnp.concatenate(
                [jnp.where(lo_half == (hh == 0), q_ref[...], 0.0).astype(BF16) * jnp.asarray(scale, BF16),
                 jnp.where(amask, qaug_ref[...], 0.0).astype(BF16)], axis=1))
        else:
            qops.append(q_ref[:, hh * LANES:(hh + 1) * LANES])

    ones_lo = jnp.where(lo_half, 1.0, 0.0).astype(BF16)
    ones_hi = jnp.where(lo_half, 0.0, 1.0).astype(BF16)

    def attend(iq_s):
        q_first = q_off + iq_s * tq
        kf = (q_first // tk) * tk
        for r in reversed(range(tq // sub)):
            rsl = slice(r * sub, (r + 1) * sub)
            nd = (r + 1) * sub if tq == tk else tk
            nk = kf + nd
            qpos = q_first + r * sub + lax.broadcasted_iota(I32, (sub, nd), 0)
            kpos = kf + lax.broadcasted_iota(I32, (sub, nd), 1)
            if fox:
                allowed = kpos <= qpos
            else:
                allowed = (lax.shift_right_logical(kpos, CHUNK_LOG2)
                           <= lax.shift_right_logical(qpos, CHUNK_LOG2))
            ps = []
            for hh in range(2):
                if fox:
                    kop = jnp.concatenate([k_ref[:nk, :], kaug_ref[:nk, :]], axis=1)
                else:
                    kop = k_ref[:nk, hh * LANES:(hh + 1) * LANES]
                s = lax.dot_general(qops[hh][rsl], kop, (((1,), (1,)), ((), ())), preferred_element_type=F32)
                s_last = jnp.where(allowed, s[:, kf:], NEG_INF)
                s = s_last if kf == 0 else jnp.concatenate([s[:, :kf], s_last], axis=1)
                m = jnp.max(s, axis=1, keepdims=True)
                ps.append(jnp.exp(s - m).astype(BF16))
            v = v_ref[:nk, :]
            vop = jnp.concatenate(
                [jnp.concatenate([jnp.where(lo_half, v, 0.0).astype(BF16), jnp.broadcast_to(ones_lo, v.shape)], axis=1),
                 jnp.concatenate([jnp.where(lo_half, 0.0, v).astype(BF16), jnp.broadcast_to(ones_hi, v.shape)], axis=1)],
                axis=0)
            acc = _dot(jnp.concatenate(ps, axis=1), vop)
            o_ref[rsl, :] = (acc[:, :LANES] / acc[:, LANES:]).astype(o_ref.dtype)

    for iq_s in range(nq):
        pl.when(iq == iq_s)(functools.partial(attend, iq_s))


def _attention(fox, q_arr, k_arr, v_arr, aug, *, n_batch, n_pairs, t_q, t_k, tq, tk, q_off,
               q_row0, q_col0, k_col0, v_col0, out_rows, name):
    assert t_q % tq == 0 and t_k % tk == 0 and q_off % tq == 0
    assert (tq == tk and q_off % tk == 0) or tk == t_k
    nq = t_q // tq
    qw = LANES if fox else 2 * LANES
    qb0 = q_row0 // tq
    assert q_row0 % tq == 0
    in_specs = [
        pl.BlockSpec((tq, qw), lambda b, p, i: (qb0 + b * nq + i, q_col0 + p)),
        pl.BlockSpec((t_k, qw), lambda b, p, i: (b, k_col0 + p)),
        pl.BlockSpec((t_k, LANES), lambda b, p, i: (b, v_col0 + p)),
    ]
    args = [q_arr, k_arr, v_arr]
    if fox:
        qaug, kaug = aug
        ab0 = q_off // tq
        in_specs += [pl.BlockSpec((None, tq, LANES), lambda b, p, i: (b, ab0 + i, 0)),
                     pl.BlockSpec((None, t_k, LANES), lambda b, p, i: (b, 0, 0))]
        args += [qaug, kaug]
    return pl.pallas_call(
        functools.partial(_attn_kernel, fox, tq, tk, q_off, nq),
        out_shape=jax.ShapeDtypeStruct((out_rows, n_pairs * LANES), BF16),
        grid=(n_batch, n_pairs, nq),
        in_specs=in_specs,
        out_specs=pl.BlockSpec((tq, LANES), lambda b, p, i: (b * nq + i, p)),
        compiler_params=_cparams(3, 48),
        name=name,
    )(*args)


def _to_row_tiles(ref, val):
    rows = val.shape[0]
    for c in range(SUBLANES):
        ref[pl.ds(c, rows, stride=SUBLANES), :] = val[:, c * LANES:(c + 1) * LANES]


def _from_row_tiles(ref):
    rows = ref.shape[0] // SUBLANES
    return jnp.concatenate([ref[pl.ds(c, rows, stride=SUBLANES), :] for c in range(SUBLANES)], axis=1)


def _proj_ln_kernel(nblk_p, oa_ref, ob_ref, xa_ref, xb_ref, w_ref, g_ref, b_ref, out_ref, out3_ref):
    is_p = pl.program_id(0) < nblk_p
    o = jnp.where(is_p, oa_ref[...], ob_ref[...])
    x = jnp.where(is_p, xa_ref[...], xb_ref[...])
    z = DEEPNORM_ALPHA * x + _dot(o, w_ref[...])
    out = _layernorm(z, g_ref[...], b_ref[...])
    out_ref[...] = out
    _to_row_tiles(out3_ref, out)


def _proj_ln(o_p, o_s, x_pair, w_o, g, b):
    n_p, d = o_p.shape
    n_s = o_s.shape[0]
    n_t = n_p + n_s
    tm = ROW_TILE
    nblk_p = n_p // tm
    if isinstance(x_pair, tuple):
        xa, xb = x_pair
        xb_spec = _rows_b(tm, d, nblk_p, 0)
    else:
        xa = xb = x_pair
        xb_spec = _rows_b(tm, d, nblk_p, nblk_p)
    return pl.pallas_call(
        functools.partial(_proj_ln_kernel, nblk_p),
        out_shape=(jax.ShapeDtypeStruct((n_t, d), F32),
                   jax.ShapeDtypeStruct((n_t * SUBLANES, LANES), F32)),
        grid=(n_t // tm,),
        in_specs=[_rows_a(tm, d, nblk_p), _rows_b(tm, d, nblk_p, 0),
                  _rows_a(tm, d, nblk_p), xb_spec,
                  _full((d, d)), _full((1, d)), _full((1, d))],
        out_specs=(_rows(tm, d), _rows(tm * SUBLANES, LANES)),
        compiler_params=_cparams(1, 32),
        name="proj_ln",
    )(o_p, o_s, xa, xb, w_o.astype(BF16), g.reshape(1, d), b.reshape(1, d))


def _router_kernel(x_ref, wc_ref, b_ref, ti_ref, tg_ref, cnt_ref):
    xh, xl = _split2(x_ref[...])
    a = _dot(xh, wc_ref[...])
    logits = a[:, :LANES] + a[:, LANES:] + _dot(xl, wc_ref[:, :LANES]) + b_ref[...]
    lane = lax.broadcasted_iota(I32, logits.shape, 1).astype(F32)
    vals, idxs = [], []
    cur = logits
    for _ in range(TOP_K):
        m = jnp.max(cur, axis=1, keepdims=True)
        idx = jnp.min(jnp.where(cur == m, lane, float(LANES)), axis=1, keepdims=True)
        vals.append(m)
        idxs.append(idx)
        cur = jnp.where(lane == idx, -jnp.inf, cur)
    exps = [jnp.exp(v - vals[0]) for v in vals]
    denom = exps[0]
    for e in exps[1:]:
        denom = denom + e
    ti = jnp.zeros(logits.shape, F32)
    tg = jnp.zeros(logits.shape, F32)
    for k in range(TOP_K):
        ti = jnp.where(lane == float(k), idxs[k], ti)
        tg = jnp.where(lane == float(k), exps[k] / denom, tg)
    ti_ref[...] = ti.astype(I32)
    tg_ref[...] = tg

    hot = jnp.zeros(logits.shape, F32)
    for k in range(TOP_K):
        hot = hot + jnp.where(lane == idxs[k], 1.0, 0.0)

    @pl.when(pl.program_id(0) == 0)
    def _():
        cnt_ref[...] = jnp.zeros(cnt_ref.shape, F32)

    cnt_ref[...] += jnp.sum(hot, axis=0, keepdims=True)


def _router(x, w_r, b_r):
    n_t, d = x.shape
    e = w_r.shape[-1]
    tm = 2 * ROW_TILE if n_t % (2 * ROW_TILE) == 0 else ROW_TILE
    w_pad = jnp.zeros((d, LANES), F32).at[:, :e].set(w_r)
    b_pad = jnp.full((1, LANES), -jnp.inf, F32).at[0, :e].set(b_r)
    wh, wl = _split2(w_pad)
    return pl.pallas_call(
        _router_kernel,
        out_shape=(jax.ShapeDtypeStruct((n_t, LANES), I32), jax.ShapeDtypeStruct((n_t, LANES), F32),
                   jax.ShapeDtypeStruct((8, LANES), F32)),
        grid=(n_t // tm,),
        in_specs=[_rows(tm, d), _full((d, 2 * LANES)), _full((1, LANES))],
        out_specs=(_rows(tm, LANES), _rows(tm, LANES), _full((8, LANES))),
        compiler_params=_cparams(1, 32),
        name="router",
    )(x, jnp.concatenate([wh, wl], axis=1), b_pad)


def _moe_plan(top_i, counts):
    n = top_i.shape[0]
    n_experts = counts.shape[0]
    blk = MOE_BLOCK
    p = n * TOP_K
    assert p % blk == 0
    bits = max(p - 1, 1).bit_length()
    assert n_experts << bits < 2 ** 31
    packed = top_i.reshape(p) * (1 << bits) + jnp.arange(p, dtype=I32)
    order = jnp.sort(packed) & ((1 << bits) - 1)
    starts = jnp.cumsum(counts) - counts
    nblk_e = (counts + blk - 1) // blk
    bends = jnp.cumsum(nblk_e)
    n_blocks = p // blk + n_experts
    blk_ids = jnp.arange(n_blocks, dtype=I32)
    block_e = jnp.minimum(jnp.sum((bends[None, :] <= blk_ids[:, None]).astype(I32), axis=1), n_experts - 1)
    b_first = (blk_ids - (bends - nblk_e)[block_e]) * blk
    b_cnt = counts[block_e]
    b_start = starts[block_e]
    r_in = jnp.arange(blk, dtype=I32)[None, :]
    off = b_first[:, None] + r_in
    valid = off < b_cnt[:, None]
    pair = order[jnp.clip(b_start[:, None] + off, 0, p - 1)]
    row = blk_ids[:, None] * blk + r_in
    pad_idx = row - (b_start + b_cnt)[:, None]
    src_tok = jnp.where(valid, pair // TOP_K, 0)
    dst_row = jnp.where(valid, (pair % TOP_K) * n + pair // TOP_K, p + pad_idx)
    return block_e, src_tok.reshape(n_blocks, 1, blk), dst_row.reshape(n_blocks, 1, blk)


def _moe_kernel(blk, nb, be_ref, *refs):
    ring = MOE_RING
    src_refs = refs[:ring]
    dstp_ref, x_hbm, wgu_ref, bgu_ref, wdn_ref, bdn_ref, y_hbm = refs[ring:ring + 7]
    xbuf = refs[ring + 7:2 * ring + 7]
    ybuf = refs[2 * ring + 7:3 * ring + 7]
    wgu16, wdn16, gsem, ssem = refs[3 * ring + 7:]
    i = pl.program_id(0)
    spare0 = nb * blk

    def gather_copy(tok, s, r):
        return pltpu.make_async_copy(x_hbm.at[tok], xbuf[s].at[pl.ds(r * SUBLANES, SUBLANES)], gsem.at[s])

    def scatter_copy(row, s, r):
        return pltpu.make_async_copy(ybuf[s].at[pl.ds(r * SUBLANES, SUBLANES)], y_hbm.at[row], ssem.at[s])

    @pl.when(i == 0)
    def _():
        for yb in ybuf:
            yb[...] = jnp.zeros(yb.shape, F32)
        for s in range(ring - 1):
            for r in range(blk):
                gather_copy(src_refs[s][0, r], s, r).start()
        for s in range(ring - 1):
            for r in range(blk):
                scatter_copy(spare0 + s * blk + r, s, r).start()

    changed = jnp.logical_or(i == 0, be_ref[jnp.minimum(i, nb - 1)] != be_ref[jnp.clip(i - 1, 0, nb - 1)])

    @pl.when(jnp.logical_and(i < nb, changed))
    def _():
        wgu16[...] = wgu_ref[...].astype(BF16)
        wdn16[...] = wdn_ref[...].astype(BF16)

    def block_step(slot):
        far = (slot + ring - 1) % ring
        for r in range(blk):
            gather_copy(0, slot, r).wait()
        for r in range(blk):
            scatter_copy(0, slot, r).wait()
        for r in range(blk):
            gather_copy(src_refs[ring - 1][0, r], far, r).start(priority=r % 2)
        for r in range(blk):
            scatter_copy(dstp_ref[0, r], far, r).start(priority=r % 2)
        f = wdn16.shape[0]
        x = _from_row_tiles(xbuf[slot]).astype(BF16)
        gu = _dot(x, wgu16[...]) + bgu_ref[...]
        g = jnp.minimum(gu[:, :f], SWIGLU_LIMIT)
        u = jnp.clip(gu[:, f:], -SWIGLU_LIMIT, SWIGLU_LIMIT)
        hid = g * jax.nn.sigmoid(SWIGLU_ALPHA * g) * (u + 1.0)
        _to_row_tiles(ybuf[slot], _dot(hid.astype(BF16), wdn16[...]) + bdn_ref[...])

    for slot in range(ring):
        pl.when(jnp.logical_and(i < nb, i % ring == slot))(functools.partial(block_step, slot))

    @pl.when(i == nb)
    def _():
        for k in range(ring - 1):
            s = (nb + k) % ring
            for r in range(blk):
                gather_copy(0, s, r).wait()
            for r in range(blk):
                scatter_copy(0, s, r).wait()
        last = (nb + ring - 1) % ring
        for r in range(blk):
            scatter_copy(dstp_ref[0, r], last, r).start()
        for r in range(blk):
            scatter_copy(0, last, r).wait()


def _moe(x3, layer, block_e, src_tok, dst_row, w_gu, b_gu, w_dn, b_dn):
    _, n_e, d, f2 = w_gu.shape
    assert x3.shape[1:] == (SUBLANES, LANES) and d == SUBLANES * LANES
    f = f2 // 2
    blk = MOE_BLOCK
    nb = src_tok.shape[0]
    assert nb >= MOE_RING
    last = nb - 1
    spare_last = (nb + MOE_RING - 1) * blk + jnp.arange(blk, dtype=I32).reshape(1, 1, blk)
    dst_late = jnp.concatenate([spare_last, dst_row], axis=0)
    src_specs = [pl.BlockSpec((None, 1, blk), functools.partial(lambda k, i, be: (jnp.minimum(i + k, last), 0, 0), k),
                              memory_space=pltpu.SMEM) for k in range(MOE_RING)]
    grid_spec = pltpu.PrefetchScalarGridSpec(
        num_scalar_prefetch=1,
        grid=(nb + 1,),
        in_specs=src_specs + [
            pl.BlockSpec((None, 1, blk), lambda i, be: (i, 0, 0), memory_space=pltpu.SMEM),
            pl.BlockSpec(memory_space=pl.ANY),
            pl.BlockSpec((None, None, d, f2), lambda i, be: (layer, be[jnp.minimum(i, last)], 0, 0)),
            pl.BlockSpec((None, None, 1, f2), lambda i, be: (layer, be[jnp.minimum(i, last)], 0, 0)),
            pl.BlockSpec((None, None, f, d), lambda i, be: (layer, be[jnp.minimum(i, last)], 0, 0)),
            pl.BlockSpec((None, None, 1, d), lambda i, be: (layer, be[jnp.minimum(i, last)], 0, 0)),
        ],
        out_specs=pl.BlockSpec(memory_space=pl.ANY),
        scratch_shapes=[pltpu.VMEM((blk * SUBLANES, LANES), F32)] * (2 * MOE_RING) + [
                        pltpu.VMEM((d, f2), BF16), pltpu.VMEM((f, d), BF16),
                        pltpu.SemaphoreType.DMA((MOE_RING,)), pltpu.SemaphoreType.DMA((MOE_RING,))],
    )
    return pl.pallas_call(
        functools.partial(_moe_kernel, blk, nb),
        out_shape=jax.ShapeDtypeStruct(((nb + MOE_RING) * blk, SUBLANES, LANES), F32),
        grid_spec=grid_spec,
        compiler_params=_cparams(1, 56),
        name="moe_experts",
    )(block_e, *([src_tok] * MOE_RING), dst_late, x3, w_gu, b_gu.reshape(b_gu.shape[0], n_e, 1, f2),
      w_dn, b_dn.reshape(b_dn.shape[0], n_e, 1, d))


def _combine_ln_kernel(nblk_p, split, *refs):
    y_refs = refs[:TOP_K]
    tg_ref, x_ref, g_ref, b_ref = refs[TOP_K:TOP_K + 4]
    out_refs = refs[TOP_K + 4:]
    tg = tg_ref[...]
    y = tg[:, 0:1] * _from_row_tiles(y_refs[0])
    for k in range(1, TOP_K):
        y = y + tg[:, k:k + 1] * _from_row_tiles(y_refs[k])
    out = _layernorm(DEEPNORM_ALPHA * x_ref[...] + y, g_ref[...], b_ref[...])
    if split:
        is_p = pl.program_id(0) < nblk_p

        @pl.when(is_p)
        def _():
            out_refs[0][...] = out

        @pl.when(jnp.logical_not(is_p))
        def _():
            out_refs[1][...] = out
    else:
        out_refs[0][...] = out


def _combine_ln(y_rows, tg, x, g, b, n_p, split):
    n_t, d = x.shape
    tm = ROW_TILE
    nblk_p = n_p // tm
    nblk = n_t // tm
    y_rows = y_rows.reshape(-1, LANES)
    y_specs = [pl.BlockSpec((tm * SUBLANES, LANES), functools.partial(lambda k, i: (k * nblk + i, 0), k))
               for k in range(TOP_K)]
    if split:
        out_shape = (jax.ShapeDtypeStruct((n_p, d), F32), jax.ShapeDtypeStruct((n_t - n_p, d), F32))
        out_specs = (_rows_a(tm, d, nblk_p), _rows_b(tm, d, nblk_p, 0))
    else:
        out_shape = jax.ShapeDtypeStruct((n_t, d), F32)
        out_specs = _rows(tm, d)
    return pl.pallas_call(
        functools.partial(_combine_ln_kernel, nblk_p, split),
        out_shape=out_shape,
        grid=(n_t // tm,),
        in_specs=y_specs + [_rows(tm, LANES), _rows(tm, d), _full((1, d)), _full((1, d))],
        out_specs=out_specs,
        compiler_params=_cparams(1, 32),
        name="combine_ln",
    )(*([y_rows] * TOP_K), tg, x, g.reshape(1, d), b.reshape(1, d))


def _moe_layer(x, x3, layer, n_p, split, ln_g, ln_b, w_r, b_r, w_gu, b_gu, w_dn, b_dn):
    n_e = w_r.shape[-1]
    ti, tg, cnt = _router(x, w_r[layer], b_r[layer])
    block_e, src_tok, dst_row = _moe_plan(ti[:, :TOP_K], cnt[0, :n_e].astype(I32))
    y_rows = _moe(x3.reshape(-1, SUBLANES, LANES), layer, block_e, src_tok, dst_row, w_gu, b_gu, w_dn, b_dn)
    return _combine_ln(y_rows, tg, x, ln_g, ln_b, n_p, split)


def _mla_proj_kernel(nblk_p, q_lora, kv_lora, n_heads, x_ref, w1_ref, gq_ref, w2a_ref, w2b_ref, gkv_ref,
                     qc_ref, qs_ref, kc_ref, ks_ref,
                     q_ref, ckvp_ref, ckvs_ref, kpep_ref, kpes_ref, ckv16_ref, kpe16_ref):
    is_p = pl.program_id(0) < nblk_p
    x = x_ref[...].astype(BF16)
    y1 = _dot(x, w1_ref[...])
    a = _rmsnorm(y1[:, :q_lora], gq_ref[...]).astype(BF16)
    qa = _dot(a, w2a_ref[...])
    qb = _dot(a, w2b_ref[...])
    qc = qc_ref[...]
    qs = qs_ref[...]
    for h in range(n_heads):
        cols = slice(h * LANES, (h + 1) * LANES)
        q_ref[:, cols] = (qa[:, cols] * qc + qb[:, cols] * qs).astype(BF16)
    ckv = _rmsnorm(y1[:, q_lora:q_lora + kv_lora], gkv_ref[...])
    c0 = q_lora + kv_lora
    kpe = y1[:, c0:c0 + LANES] * kc_ref[...] + y1[:, c0 + LANES:c0 + 2 * LANES] * ks_ref[...]
    ckv16_ref[...] = ckv.astype(BF16)
    kpe16_ref[...] = kpe.astype(BF16)
    rope = kpep_ref.shape[-1]

    @pl.when(is_p)
    def _():
        ckvp_ref[...] = ckv
        kpep_ref[...] = kpe[:, :rope]

    @pl.when(jnp.logical_not(is_p))
    def _():
        ckvs_ref[...] = ckv
        kpes_ref[...] = kpe[:, :rope]


def _rot_cols(w):
    half = w.shape[-1] // 2
    return jnp.concatenate([-w[..., half:], w[..., :half]], axis=-1)


def _mla_proj(x, n_p, t_p, t_s, past, w_dq, g_q, w_uq, w_dkv, g_kv):
    n_t, d = x.shape
    n_s = n_t - n_p
    tm = ROW_TILE
    nblk_p = n_p // tm
    q_lora = w_dq.shape[-1]
    kv_lora = g_kv.shape[-1]
    hd = MLA_NOPE + MLA_ROPE
    n_heads = w_uq.shape[-1] // hd
    assert q_lora % LANES == 0 and kv_lora % LANES == 0 and t_p % tm == 0 and tm % t_s == 0

    w_pe = w_dkv[:, kv_lora:]
    pad = jnp.zeros((d, LANES - MLA_ROPE), F32)
    w1 = jnp.concatenate([w_dq, w_dkv[:, :kv_lora], w_pe, pad, _rot_cols(w_pe), pad], axis=1).astype(BF16)

    w3 = w_uq.reshape(q_lora, n_heads, hd)
    zq = jnp.zeros((q_lora, n_heads, LANES - hd), F32)
    w2a = jnp.concatenate([w3, zq], axis=2).reshape(q_lora, n_heads * LANES).astype(BF16)
    w2b = jnp.concatenate([jnp.zeros((q_lora, n_heads, MLA_NOPE), F32), _rot_cols(w3[:, :, MLA_NOPE:]), zq],
                          axis=2).reshape(q_lora, n_heads * LANES).astype(BF16)

    half = MLA_ROPE // 2
    inv_freq = ROPE_THETA ** (-jnp.arange(half, dtype=F32) / half)
    pos = jnp.concatenate([jnp.arange(t_p), past + (jnp.arange(tm) % t_s)]).astype(F32)
    ang = pos[:, None] * inv_freq[None, :]
    cos2 = jnp.concatenate([jnp.cos(ang), jnp.cos(ang)], axis=1)
    sin2 = jnp.concatenate([jnp.sin(ang), jnp.sin(ang)], axis=1)
    rows = pos.shape[0]
    scale = hd ** -0.5
    z32 = jnp.zeros((rows, LANES - hd), F32)
    q_cos = jnp.concatenate([jnp.full((rows, MLA_NOPE), scale, F32), scale * cos2, z32], axis=1)
    q_sin = jnp.concatenate([jnp.zeros((rows, MLA_NOPE), F32), scale * sin2, z32], axis=1)
    z96 = jnp.zeros((rows, LANES - MLA_ROPE), F32)
    k_cos = jnp.concatenate([cos2, z96], axis=1)
    k_sin = jnp.concatenate([sin2, z96], axis=1)
    tpb = t_p // tm
    tbl = pl.BlockSpec((tm, LANES), lambda i: (jnp.where(i < nblk_p, i % tpb, tpb), 0))

    w1w = w1.shape[1]
    return pl.pallas_call(
        functools.partial(_mla_proj_kernel, nblk_p, q_lora, kv_lora, n_heads),
        out_shape=(jax.ShapeDtypeStruct((n_t, n_heads * LANES), BF16),
                   jax.ShapeDtypeStruct((n_p, kv_lora), F32), jax.ShapeDtypeStruct((n_s, kv_lora), F32),
                   jax.ShapeDtypeStruct((n_p, MLA_ROPE), F32), jax.ShapeDtypeStruct((n_s, MLA_ROPE), F32),
                   jax.ShapeDtypeStruct((n_t, kv_lora), BF16), jax.ShapeDtypeStruct((n_t, LANES), BF16)),
        grid=(n_t // tm,),
        in_specs=[_rows(tm, d), _full((d, w1w)), _full((1, q_lora)),
                  _full((q_lora, n_heads * LANES)), _full((q_lora, n_heads * LANES)), _full((1, kv_lora)),
                  tbl, tbl, tbl, tbl],
        out_specs=(_rows(tm, n_heads * LANES),
                   _rows_a(tm, kv_lora, nblk_p), _rows_b(tm, kv_lora, nblk_p, 0),
                   _rows_a(tm, MLA_ROPE, nblk_p), _rows_b(tm, MLA_ROPE, nblk_p, 0),
                   _rows(tm, kv_lora), _rows(tm, LANES)),
        compiler_params=_cparams(1, 48),
        name="mla_proj",
    )(x, w1, g_q.reshape(1, q_lora), w2a, w2b, g_kv.reshape(1, kv_lora), q_cos, q_sin, k_cos, k_sin)


def _mla_expand_kernel(ckv_ref, kpe_ref, wk_ref, pk_ref, wv_ref, k_ref, v_ref):
    ckv = ckv_ref[...]
    k_ref[...] = (_dot(ckv, wk_ref[...]) + _dot(kpe_ref[...], pk_ref[...])).astype(BF16)
    v_ref[...] = _dot(ckv, wv_ref[...]).astype(BF16)


def _mla_expand(ckv16, kpe16, n_rows, w_ukv):
    kv_lora = ckv16.shape[-1]
    hd = MLA_NOPE + MLA_V
    n_heads = w_ukv.shape[-1] // hd
    tm = next(t for t in (512, 256, LANES) if n_rows % t == 0)
    w3 = w_ukv.reshape(kv_lora, n_heads, hd)
    wk = jnp.concatenate([w3[:, :, :MLA_NOPE], jnp.zeros((kv_lora, n_heads, LANES - MLA_NOPE), F32)],
                         axis=2).reshape(kv_lora, n_heads * LANES).astype(BF16)
    wv = w3[:, :, MLA_NOPE:].reshape(kv_lora, n_heads * MLA_V).astype(BF16)
    pk = np.zeros((LANES, n_heads, LANES), np.float32)
    for j in range(MLA_ROPE):
        pk[j, :, MLA_NOPE + j] = 1.0
    pk = jnp.asarray(pk.reshape(LANES, n_heads * LANES), BF16)
    return pl.pallas_call(
        _mla_expand_kernel,
        out_shape=(jax.ShapeDtypeStruct((n_rows, n_heads * LANES), BF16),
                   jax.ShapeDtypeStruct((n_rows, n_heads * MLA_V), BF16)),
        grid=(n_rows // tm,),
        in_specs=[_rows(tm, kv_lora), _rows(tm, LANES), _full((kv_lora, n_heads * LANES)),
                  _full((LANES, n_heads * LANES)), _full((kv_lora, n_heads * MLA_V))],
        out_specs=(_rows(tm, n_heads * LANES), _rows(tm, n_heads * MLA_V)),
        compiler_params=_cparams(1, 32),
        name="mla_expand",
    )(ckv16, kpe16, wk, pk, wv)


def _pad_keys(cache, new, t_pad):
    b, t_c, w = cache.shape
    t_n = new.shape[1]
    parts = [cache.astype(BF16), new.astype(BF16)]
    if t_pad > t_c + t_n:
        parts.append(jnp.zeros((b, t_pad - t_c - t_n, w), BF16))
    return jnp.concatenate(parts, axis=1).reshape(b * t_pad, w)


def kernel(x_prompt, x_sample, cache_fox_k, cache_fox_v, cache_fox_logf, cache_mla_ckv, cache_mla_kpe, fox_w_qkv, fox_w_f, fox_b_f, fox_w_o, mla_w_dq, mla_g_q, mla_w_uq, mla_w_dkv, mla_g_kv, mla_w_ukv, mla_w_o, ln_g, ln_b, moe_w_router, moe_b_router, moe_w_gu, moe_b_gu, moe_w_down, moe_b_down):
    b_p, t_p, d = x_prompt.shape
    b_s, t_s, _ = x_sample.shape
    past = cache_fox_k.shape[2]
    n_p, n_s = b_p * t_p, b_s * t_s
    assert fox_w_qkv.shape[0] == 1 and mla_w_dq.shape[0] == 1 and ln_g.shape[0] == DEPTH
    tq_p = min(ATTN_TILE, t_p)
    assert n_p % ROW_TILE == 0 and n_s % ROW_TILE == 0 and t_p % tq_p == 0 and tq_p % ATTN_SUB == 0
    xp = x_prompt.reshape(n_p, d)
    xs = x_sample.reshape(n_s, d)
    moe_w = (moe_w_router, moe_b_router, moe_w_gu, moe_b_gu, moe_w_down, moe_b_down)
    t_all = -(-(past + t_s) // LANES) * LANES

    n_fh = fox_w_f.shape[-1]
    fd = d // n_fh
    assert 2 * fd == LANES
    qkv16, k_p, k_s, v_p, v_s, logf = _fox_proj(xp, xs, fox_w_qkv[0], fox_w_f[0], fox_b_f[0])
    logf_p = logf[:n_p].reshape(b_p, t_p, n_fh)
    logf_s = logf[n_p:].reshape(b_s, t_s, n_fh)
    qaug_p, kaug_p = _fox_aug(logf_p)
    logf_all = jnp.concatenate([cache_fox_logf[0].astype(F32), logf_s,
                                jnp.zeros((b_s, t_all - past - t_s, n_fh), F32)], axis=1)
    qaug_s, kaug_s = _fox_aug(logf_all)
    n_pairs = n_fh // 2
    o_p = _attention(True, qkv16, qkv16, qkv16, (qaug_p, kaug_p), n_batch=b_p, n_pairs=n_pairs,
                     t_q=t_p, t_k=t_p, tq=tq_p, tk=tq_p, q_off=0, q_row0=0,
                     q_col0=0, k_col0=n_pairs, v_col0=2 * n_pairs, out_rows=n_p, name="fox_attn_prompt")
    k_all = _pad_keys(cache_fox_k[0].reshape(b_s, past, d), k_s.reshape(b_s, t_s, d), t_all)
    v_all = _pad_keys(cache_fox_v[0].reshape(b_s, past, d), v_s.reshape(b_s, t_s, d), t_all)
    o_s = _attention(True, qkv16, k_all, v_all, (qaug_s, kaug_s), n_batch=b_s, n_pairs=n_pairs,
                     t_q=t_s, t_k=t_all, tq=t_s, tk=t_all, q_off=past, q_row0=n_p,
                     q_col0=0, k_col0=0, v_col0=0, out_rows=n_s, name="fox_attn_sample")
    x1, x1_rows = _proj_ln(o_p, o_s, (xp, xs), fox_w_o[0], ln_g[0, 0], ln_b[0, 0])
    x2 = _moe_layer(x1, x1_rows, 0, n_p, False, ln_g[0, 1], ln_b[0, 1], *moe_w)

    q16, ckv_p, ckv_s, kpe_p, kpe_s, ckv16, kpe16 = _mla_proj(
        x2, n_p, t_p, t_s, past, mla_w_dq[0], mla_g_q[0], mla_w_uq[0], mla_w_dkv[0], mla_g_kv[0])
    kv_lora = ckv_p.shape[-1]
    n_mh = mla_w_o.shape[1] // MLA_V
    m_pairs = n_mh // 2
    k16_p, v16_p = _mla_expand(ckv16, kpe16, n_p, mla_w_ukv[0])
    o_p = _attention(False, q16, k16_p, v16_p, None, n_batch=b_p, n_pairs=m_pairs,
                     t_q=t_p, t_k=t_p, tq=tq_p, tk=tq_p, q_off=0, q_row0=0,
                     q_col0=0, k_col0=0, v_col0=0, out_rows=n_p, name="mla_attn_prompt")
    ckv_all = _pad_keys(cache_mla_ckv[0], ckv16[n_p:].reshape(b_s, t_s, kv_lora), t_all)
    kpe_cache = jnp.pad(cache_mla_kpe[0], ((0, 0), (0, 0), (0, LANES - MLA_ROPE)))
    kpe_all = _pad_keys(kpe_cache, kpe16[n_p:].reshape(b_s, t_s, LANES), t_all)
    k16_s, v16_s = _mla_expand(ckv_all, kpe_all, b_s * t_all, mla_w_ukv[0])
    o_s = _attention(False, q16, k16_s, v16_s, None, n_batch=b_s, n_pairs=m_pairs,
                     t_q=t_s, t_k=t_all, tq=t_s, tk=t_all, q_off=past, q_row0=n_p,
                     q_col0=0, k_col0=0, v_col0=0, out_rows=n_s, name="mla_attn_sample")
    x3, x3_rows = _proj_ln(o_p, o_s, x2, mla_w_o[0], ln_g[1, 0], ln_b[1, 0])
    y_p, y_s = _moe_layer(x3, x3_rows, 1, n_p, True, ln_g[1, 1], ln_b[1, 1], *moe_w)

    return (y_p.reshape(b_p, t_p, d), y_s.reshape(b_s, t_s, d),
            k_p.reshape(1, b_p, t_p, n_fh, fd), v_p.reshape(1, b_p, t_p, n_fh, fd),
            logf_p.reshape(1, b_p, t_p, n_fh),
            ckv_p.reshape(1, b_p, t_p, kv_lora), kpe_p.reshape(1, b_p, t_p, MLA_ROPE),
            k_s.reshape(1, b_s, t_s, n_fh, fd), v_s.reshape(1, b_s, t_s, n_fh, fd),
            logf_s.reshape(1, b_s, t_s, n_fh),
            ckv_s.reshape(1, b_s, t_s, kv_lora), kpe_s.reshape(1, b_s, t_s, MLA_ROPE))
```
